```python
import math, functools
import jax, jax.numpy as jnp
from jax import lax
import numpy as np

D_MODEL = 4096
BATCH = 2
SEQ = 4096
DEPTH = 1
DEC_BATCH = 32
DEC_SEQ = 8
PAST_LEN = 8192
PAGE_SIZE = 128

HEAD_DIM = 64
D_MIX = D_MODEL
D_RWKV = D_MIX // 2
D_FOX = D_MIX - D_RWKV
H_RWKV = D_RWKV // HEAD_DIM
H_FOX = D_FOX // HEAD_DIM
DECAY_LORA = max(32, int(round(D_RWKV ** 0.5 * 1.8 / 32)) * 32)
ICLR_LORA = max(32, int(round(D_RWKV ** 0.5 * 1.8 / 32)) * 32)
GATE_LORA = max(32, int(round(D_RWKV ** 0.8 * 0.6 / 32)) * 32)
RW_COLS = 3 * D_RWKV + DECAY_LORA + ICLR_LORA + GATE_LORA
FOX_COLS = 3 * D_FOX + H_FOX
IN_COLS = RW_COLS + FOX_COLS
RW_SPLITS = (D_RWKV, 2 * D_RWKV, 3 * D_RWKV, 3 * D_RWKV + DECAY_LORA,
             3 * D_RWKV + DECAY_LORA + ICLR_LORA)
FOX_SPLITS = (D_FOX, 2 * D_FOX, 3 * D_FOX)
D_FF = int(round(8 * D_MODEL / 3 / 256)) * 256
CONV_W = 3
N_META = 16
Q_BLOCK = 128
ALPHA = (2 * DEPTH) ** 0.25
BETA = (8 * DEPTH) ** -0.25
LN_EPS = 1e-5
GN_EPS = 64e-5
NEG_INF = -1e30

kernel_name = "hymba_rwkv7_fox_convffn_deepnorm_step"


def _layernorm(x, g, b):
    xf = x.astype(jnp.float32)
    mu = jnp.mean(xf, axis=-1, keepdims=True)
    var = jnp.mean(jnp.square(xf - mu), axis=-1, keepdims=True)
    return ((xf - mu) * lax.rsqrt(var + LN_EPS)).astype(x.dtype) * g + b


def _heads(t):
    return t.reshape(t.shape[:-1] + (t.shape[-1] // HEAD_DIM, HEAD_DIM))


def _rwkv7(p_rw, shift_prev, s0, mu_shift, w0, w_decay_up, a0, w_iclr_up, w_gate_up,
           k_k, k_a, r_k, gn_g, gn_b):
    f32 = jnp.float32
    b, n = p_rw.shape[:2]
    p_prev = jnp.concatenate([shift_prev[:, None].astype(p_rw.dtype), p_rw[:, :-1]], axis=1)
    xs = p_rw + (p_prev - p_rw) * mu_shift
    r, k, v, xw, xa, xg = jnp.split(xs, RW_SPLITS, axis=-1)
    w_log = -jax.nn.softplus(-(w0 + jnp.tanh(xw) @ w_decay_up).astype(f32)) - 0.5
    decay = jnp.exp(-jnp.exp(w_log))
    a = jax.nn.sigmoid(a0 + xa @ w_iclr_up)
    g = jax.nn.sigmoid(xg) @ w_gate_up
    kk = _heads(k * k_k).astype(f32)
    kk = kk / jnp.maximum(jnp.linalg.norm(kk, axis=-1, keepdims=True), 1e-12)
    k = k * (1 + (a - 1) * k_a)
    r, k, v, a, decay = (_heads(t) for t in (r, k, v, a, decay))

    def step(s, inp):
        r_t, w_t, k_t, v_t, kk_t, a_t = inp
        s_kk = jnp.einsum('bhvk,bhk->bhv', s, kk_t)
        s = (s * w_t[:, :, None, :]
             - s_kk[..., None] * (kk_t * a_t)[:, :, None, :]
             + v_t[..., None] * k_t[:, :, None, :])
        return s, jnp.einsum('bhvk,bhk->bhv', s, r_t)

    seq = tuple(jnp.moveaxis(t.astype(f32), 1, 0) for t in (r, decay, k, v, kk, a))
    s_last, o = lax.scan(step, s0.astype(f32), seq)
    o = jnp.moveaxis(o, 0, 1)
    mu = jnp.mean(o, axis=-1, keepdims=True)
    var = jnp.mean(jnp.square(o - mu), axis=-1, keepdims=True)
    o = ((o - mu) * lax.rsqrt(var + GN_EPS)).reshape(b, n, D_RWKV).astype(p_rw.dtype) * gn_g + gn_b
    bonus = jnp.sum(r * k * r_k, axis=-1, keepdims=True) * v
    o = (o + bonus.reshape(b, n, D_RWKV)) * g
    return o, s_last.astype(s0.dtype), p_rw[:, -1]


def _fox_prompt(q, k, v, logf):
    b, t, h = q.shape[:3]
    pad = (-t) % Q_BLOCK
    tp = t + pad
    n_blocks = tp // Q_BLOCK

    def front_pad(arr):
        return jnp.pad(arr, [(0, 0), (pad, 0)] + [(0, 0)] * (arr.ndim - 2))

    qp, kp, vp, lfp = (front_pad(arr) for arr in (q, k, v, logf))
    c = jnp.cumsum(lfp, axis=1)
    c_keys = jnp.transpose(c, (0, 2, 1))
    key_pos = jnp.arange(tp)
    q_blocks = jnp.moveaxis(qp.reshape(b, n_blocks, Q_BLOCK, h, HEAD_DIM), 1, 0)
    c_blocks = jnp.moveaxis(c_keys.reshape(b, h, n_blocks, Q_BLOCK), 2, 0)
    scale = HEAD_DIM ** -0.5

    def block(args):
        i, q_i, c_i = args
        s = jnp.einsum('bqhd,bkhd->bhqk', q_i, kp).astype(jnp.float32) * scale
        s = s + c_i[..., None] - c_keys[:, :, None, :]
        q_pos = i * Q_BLOCK + jnp.arange(Q_BLOCK)
        valid = (key_pos[None, :] <= q_pos[:, None]) & (key_pos[None, :] >= pad)
        p = jax.nn.softmax(jnp.where(valid, s, NEG_INF), axis=-1)
        return jnp.einsum('bhqk,bkhd->bqhd', p.astype(vp.dtype), vp)

    o = lax.map(block, (jnp.arange(n_blocks), q_blocks, c_blocks))
    return jnp.moveaxis(o, 0, 1).reshape(b, tp, h, HEAD_DIM)[:, pad:]


def _fox_sample(q, k, v, logf, cache_k, cache_v, cache_logf, page_table):
    f32 = jnp.float32
    db, nq, h = q.shape[:3]
    k_past = cache_k[page_table].reshape(db, -1, h, HEAD_DIM)
    v_past = cache_v[page_table].reshape(db, -1, h, HEAD_DIM)
    lf_past = cache_logf[page_table].reshape(db, -1, h).astype(f32)
    n_past = k_past.shape[1]
    c_past = jnp.cumsum(lf_past, axis=1)
    c_new = c_past[:, -1:] + jnp.cumsum(logf, axis=1)
    cq = jnp.transpose(c_new, (0, 2, 1))
    scale = HEAD_DIM ** -0.5
    s_past = (jnp.einsum('bqhd,bkhd->bhqk', q, k_past).astype(f32) * scale
              + cq[..., None] - jnp.transpose(c_past, (0, 2, 1))[:, :, None, :])
    s_new = (jnp.einsum('bqhd,bkhd->bhqk', q, k).astype(f32) * scale
             + cq[..., None] - cq[:, :, None, :])
    causal = jnp.tril(jnp.ones((nq, nq), dtype=bool))
    s_new = jnp.where(causal, s_new, NEG_INF)
    p = jax.nn.softmax(jnp.concatenate([s_past, s_new], axis=-1), axis=-1)
    return (jnp.einsum('bhqk,bkhd->bqhd', p[..., :n_past].astype(v_past.dtype), v_past)
            + jnp.einsum('bhqk,bkhd->bqhd', p[..., n_past:].astype(v.dtype), v))


def _conv_ffn(x, prev, w_ffn_in, conv_w, conv_b, w_ffn_out):
    n = x.shape[1]
    u, gate = jnp.split(x @ w_ffn_in, 2, axis=-1)
    ext = jnp.concatenate([prev.astype(u.dtype), u], axis=1)
    c = conv_b
    for j in range(CONV_W):
        c = c + conv_w[j] * ext[:, j:j + n]
    h = jax.nn.gelu(c, approximate=True) * gate
    return h @ w_ffn_out, ext[:, n:]


def _layer(h, attend, shift_prev, s0, ffn_prev, lw):
    (w_in, b_f, mu_shift, w0, w_decay_up, a0, w_iclr_up, w_gate_up, k_k, k_a, r_k,
     gn_g, gn_b, w_out, ln1_g, ln1_b, w_ffn_in, conv_w, conv_b, w_ffn_out, ln2_g, ln2_b) = lw
    b, n = h.shape[:2]
    proj = h @ w_in
    p_rw, p_fox = proj[..., :RW_COLS], proj[..., RW_COLS:]
    o_rw, s_last, shift_last = _rwkv7(p_rw, shift_prev, s0, mu_shift, w0, w_decay_up, a0,
                                      w_iclr_up, w_gate_up, k_k, k_a, r_k, gn_g, gn_b)
    q, k, v, f_logit = jnp.split(p_fox, FOX_SPLITS, axis=-1)
    logf = jax.nn.log_sigmoid((f_logit + b_f).astype(jnp.float32))
    k, v = _heads(k), _heads(v)
    o_fox = attend(_heads(q), k, v, logf)
    mix = jnp.concatenate([o_rw, o_fox.reshape(b, n, D_FOX)], axis=-1) @ w_out
    x1 = _layernorm(ALPHA * h + mix, ln1_g, ln1_b)
    ffn, ffn_state = _conv_ffn(x1, ffn_prev, w_ffn_in, conv_w, conv_b, w_ffn_out)
    x2 = _layernorm(ALPHA * x1 + ffn, ln2_g, ln2_b)
    return x2, (k, v, logf, s_last, shift_last, ffn_state)


def setup_inputs(seed: int = 0) -> dict:
    key = jax.random.key(seed)
    keys = iter(jax.random.split(key, 48))

    def nrm(shape, scale=1.0):
        return scale * jax.random.normal(next(keys), shape, jnp.float32)

    n_pages = PAST_LEN // PAGE_SIZE
    n_used = DEC_BATCH * n_pages
    n_pool = n_used + max(1, n_used // 4)
    page_table = jax.random.permutation(next(keys), n_pool)[:n_used].reshape(
        DEC_BATCH, n_pages).astype(jnp.int32)
    L = DEPTH
    inp = {}
    inp["x_prompt"] = nrm((BATCH, SEQ, D_MODEL))
    inp["x_sample"] = nrm((DEC_BATCH, DEC_SEQ, D_MODEL))
    inp["cache_k"] = nrm((L, n_pool, PAGE_SIZE, H_FOX, HEAD_DIM))
    inp["cache_v"] = nrm((L, n_pool, PAGE_SIZE, H_FOX, HEAD_DIM))
    inp["cache_logf"] = jax.nn.log_sigmoid(3.0 + nrm((L, n_pool, PAGE_SIZE, H_FOX)))
    inp["state_rwkv"] = nrm((L, DEC_BATCH, H_RWKV, HEAD_DIM, HEAD_DIM))
    inp["state_shift"] = nrm((L, DEC_BATCH, RW_COLS))
    inp["state_ffn_conv"] = nrm((L, DEC_BATCH, CONV_W - 1, D_FF))
    inp["page_table"] = page_table
    inp["meta_tokens"] = nrm((N_META, D_MODEL))
    inp["ln0_g"] = 1.0 + nrm((D_MODEL,), 0.02)
    inp["ln0_b"] = nrm((D_MODEL,), 0.02)
    inp["w_in"] = nrm((L, D_MODEL, IN_COLS), D_MODEL ** -0.5)
    inp["b_f"] = 3.0 + nrm((L, H_FOX), 0.5)
    inp["mu_shift"] = jax.random.uniform(next(keys), (L, RW_COLS), jnp.float32)
    inp["w0"] = -2.0 + nrm((L, D_RWKV), 0.5)
    inp["w_decay_up"] = nrm((L, DECAY_LORA, D_RWKV), 0.5 * DECAY_LORA ** -0.5)
    inp["a0"] = nrm((L, D_RWKV), 0.1)
    inp["w_iclr_up"] = nrm((L, ICLR_LORA, D_RWKV), 0.5 * ICLR_LORA ** -0.5)
    inp["w_gate_up"] = nrm((L, GATE_LORA, D_RWKV), GATE_LORA ** -0.5)
    inp["k_k"] = 0.85 + nrm((L, D_RWKV), 0.02)
    inp["k_a"] = 1.0 + nrm((L, D_RWKV), 0.02)
    inp["r_k"] = nrm((L, H_RWKV, HEAD_DIM), 0.1)
    inp["gn_g"] = 1.0 + nrm((L, D_RWKV), 0.02)
    inp["gn_b"] = nrm((L, D_RWKV), 0.02)
    inp["w_out"] = nrm((L, D_MIX, D_MODEL), BETA * D_MIX ** -0.5)
    inp["ln1_g"] = 1.0 + nrm((L, D_MODEL), 0.02)
    inp["ln1_b"] = nrm((L, D_MODEL), 0.02)
    inp["w_ffn_in"] = nrm((L, D_MODEL, 2 * D_FF), D_MODEL ** -0.5)
    inp["conv_w"] = nrm((L, CONV_W, D_FF), CONV_W ** -0.5)
    inp["conv_b"] = nrm((L, D_FF), 0.02)
    inp["w_ffn_out"] = nrm((L, D_FF, D_MODEL), BETA * D_FF ** -0.5)
    inp["ln2_g"] = 1.0 + nrm((L, D_MODEL), 0.02)
    inp["ln2_b"] = nrm((L, D_MODEL), 0.02)
    return inp


def reference(x_prompt, x_sample, cache_k, cache_v, cache_logf, state_rwkv, state_shift,
              state_ffn_conv, page_table, meta_tokens, ln0_g, ln0_b, w_in, b_f, mu_shift, w0,
              w_decay_up, a0, w_iclr_up, w_gate_up, k_k, k_a, r_k, gn_g, gn_b, w_out,
              ln1_g, ln1_b, w_ffn_in, conv_w, conv_b, w_ffn_out, ln2_g, ln2_b):
    b = x_prompt.shape[0]
    dt = x_prompt.dtype
    meta = jnp.broadcast_to(meta_tokens[None].astype(dt), (b, N_META, D_MODEL))
    h_p = _layernorm(jnp.concatenate([meta, x_prompt], axis=1), ln0_g, ln0_b)
    h_s = _layernorm(x_sample, ln0_g, ln0_b)
    zero_shift = jnp.zeros((b, RW_COLS), dt)
    zero_rwkv = jnp.zeros((b, H_RWKV, HEAD_DIM, HEAD_DIM), jnp.float32)
    zero_conv = jnp.zeros((b, CONV_W - 1, D_FF), dt)
    new_p = [[] for _ in range(6)]
    new_s = [[] for _ in range(6)]
    for l in range(DEPTH):
        lw = tuple(w[l] for w in (w_in, b_f, mu_shift, w0, w_decay_up, a0, w_iclr_up,
                                  w_gate_up, k_k, k_a, r_k, gn_g, gn_b, w_out, ln1_g, ln1_b,
                                  w_ffn_in, conv_w, conv_b, w_ffn_out, ln2_g, ln2_b))
        h_p, st_p = _layer(h_p, _fox_prompt, zero_shift, zero_rwkv, zero_conv, lw)
        attend_s = functools.partial(_fox_sample, cache_k=cache_k[l], cache_v=cache_v[l],
                                     cache_logf=cache_logf[l], page_table=page_table)
        h_s, st_s = _layer(h_s, attend_s, state_shift[l], state_rwkv[l], state_ffn_conv[l], lw)
        for acc, arr in zip(new_p, st_p):
            acc.append(arr)
        for acc, arr in zip(new_s, st_s):
            acc.append(arr)
    k_p, v_p, lf_p, rw_p, sh_p, cv_p = (jnp.stack(acc) for acc in new_p)
    k_s, v_s, lf_s, rw_s, sh_s, cv_s = (jnp.stack(acc) for acc in new_s)
    y_prompt = h_p[:, N_META:]
    y_sample = h_s
    return (y_prompt, y_sample, k_p, v_p, lf_p, rw_p, sh_p, cv_p, k_s, v_s, lf_s, rw_s, sh_s, cv_s)
```

```python
import functools

import jax
import jax.numpy as jnp
from jax import lax
from jax.experimental import pallas as pl
from jax.experimental.pallas import tpu as pltpu

F32 = jnp.float32
BF16 = jnp.bfloat16

HEAD_DIM = 64
LANES = 128
Q_BLOCK = 128
LN_EPS = 1e-5
GN_EPS = 64e-5
NEG_INF = -1e30
VMEM_LIMIT = 56 * 1024 * 1024


def _cparams(sem):
    return pltpu.CompilerParams(dimension_semantics=sem, vmem_limit_bytes=VMEM_LIMIT)


def _pick(n, candidates):
    for c in candidates:
        if n % c == 0:
            return c
    return n


def _split_bf16(x):
    hi = x.astype(BF16)
    lo = (x - hi.astype(F32)).astype(BF16)
    return hi, lo


def _dot(a, b):
    return jnp.dot(a, b, preferred_element_type=F32)


def _dot_nt(a, b):
    return lax.dot_general(a, b, (((1,), (1,)), ((), ())), preferred_element_type=F32)


def _dot_x01(x, m01):
    hi, lo = _split_bf16(x)
    return _dot(hi, m01) + _dot(lo, m01)


def _dot_01x(m01, x):
    hi, lo = _split_bf16(x)
    return _dot(m01, hi) + _dot(m01, lo)


def _idiv(x, c):
    return lax.shift_right_logical(x, c.bit_length() - 1) if c & (c - 1) == 0 else x // c


def _imod(x, c):
    return (x & (c - 1)) if c & (c - 1) == 0 else x % c


def _head_select(n_cols, n_heads):
    c = lax.broadcasted_iota(jnp.int32, (n_cols, n_heads), 0)
    h = lax.broadcasted_iota(jnp.int32, (n_cols, n_heads), 1)
    return jnp.where(_idiv(c, HEAD_DIM) == h, 1.0, 0.0).astype(BF16)


def _head_expand(n_heads, n_cols):
    h = lax.broadcasted_iota(jnp.int32, (n_heads, n_cols), 0)
    c = lax.broadcasted_iota(jnp.int32, (n_heads, n_cols), 1)
    return jnp.where(_idiv(c, HEAD_DIM) == h, 1.0, 0.0).astype(BF16)


def _head_sum_bcast(x, sel, exp):
    return _dot_x01(_dot_x01(x, sel), exp)


def _softplus(z):
    return jnp.maximum(z, 0.0) + jnp.log1p(jnp.exp(-jnp.abs(z)))


def _sigmoid(z):
    return 1.0 / (1.0 + jnp.exp(-z))


def _ln_rows(x, g, b):
    mu = jnp.mean(x, axis=-1, keepdims=True)
    xc = x - mu
    var = jnp.mean(xc * xc, axis=-1, keepdims=True)
    return xc * lax.rsqrt(var + LN_EPS) * g + b


def _ln0_kernel(x_ref, g_ref, b_ref, o_ref):
    o_ref[...] = _ln_rows(x_ref[...], g_ref[...], b_ref[...]).astype(o_ref.dtype)


def _ln0(x2, g, b):
    m, d = x2.shape
    tm = _pick(m, (528, 512, 384, 256, 128, 64, 32, 16, 8))
    return pl.pallas_call(
        _ln0_kernel,
        grid=(m // tm,),
        in_specs=[pl.BlockSpec((tm, d), lambda i: (i, 0)),
                  pl.BlockSpec((1, d), lambda i: (0, 0)),
                  pl.BlockSpec((1, d), lambda i: (0, 0))],
        out_specs=pl.BlockSpec((tm, d), lambda i: (i, 0)),
        out_shape=jax.ShapeDtypeStruct((m, d), BF16),
        compiler_params=_cparams(("parallel",)),
        name="ln0",
    )(x2, g.reshape(1, d), b.reshape(1, d))


def _ln1_kernel(x_ref, mix_ref, g0_ref, b0_ref, g_ref, b_ref, of_ref, ob_ref, *, alpha):
    h = _ln_rows(x_ref[...], g0_ref[...], b0_ref[...])
    x1 = _ln_rows(alpha * h + mix_ref[...], g_ref[...], b_ref[...])
    of_ref[...] = x1
    ob_ref[...] = x1.astype(BF16)


def _ln1(x2, mix, g0, b0, g, b, alpha):
    m, d = x2.shape
    tm = _pick(m, (264, 256, 128, 64, 32, 16, 8))
    row = pl.BlockSpec((tm, d), lambda i: (i, 0))
    vec = pl.BlockSpec((1, d), lambda i: (0, 0))
    return pl.pallas_call(
        functools.partial(_ln1_kernel, alpha=alpha),
        grid=(m // tm,),
        in_specs=[row, row, vec, vec, vec, vec],
        out_specs=[row, row],
        out_shape=[jax.ShapeDtypeStruct((m, d), F32), jax.ShapeDtypeStruct((m, d), BF16)],
        compiler_params=_cparams(("parallel",)),
        name="ln1",
    )(x2, mix, g0.reshape(1, d), b0.reshape(1, d), g.reshape(1, d), b.reshape(1, d))


def _ln2_kernel(x1_ref, ffn_ref, g_ref, b_ref, o_ref, *, alpha):
    o_ref[...] = _ln_rows(alpha * x1_ref[...] + ffn_ref[...], g_ref[...], b_ref[...])


def _ln2(x1, ffn, g, b, alpha, n_seq, seq_rows, skip_rows):
    d = x1.shape[-1]
    out_rows = seq_rows - skip_rows
    tb = next(c for c in (256, 128, 64, 32, 16, 8) if skip_rows % c == 0 and out_rows % c == 0)
    skip = skip_rows // tb
    x3 = x1.reshape(n_seq, seq_rows, d)
    f3 = ffn.reshape(n_seq, seq_rows, d)
    row_in = pl.BlockSpec((None, tb, d), lambda s, i: (s, i + skip, 0))
    vec = pl.BlockSpec((1, d), lambda s, i: (0, 0))
    return pl.pallas_call(
        functools.partial(_ln2_kernel, alpha=alpha),
        grid=(n_seq, out_rows // tb),
        in_specs=[row_in, row_in, vec, vec],
        out_specs=pl.BlockSpec((None, tb, d), lambda s, i: (s, i, 0)),
        out_shape=jax.ShapeDtypeStruct((n_seq, out_rows, d), F32),
        compiler_params=_cparams(("parallel", "parallel")),
        name="ln2",
    )(x3, f3, g.reshape(1, d), b.reshape(1, d))


def _mm_kernel(x_ref, w_ref, o_ref):
    o_ref[...] = _dot(x_ref[...], w_ref[...])


def _matmul(x, w, tm, tn, name):
    m, k = x.shape
    n = w.shape[1]
    return pl.pallas_call(
        _mm_kernel,
        grid=(m // tm, n // tn),
        in_specs=[pl.BlockSpec((tm, k), lambda i, j: (i, 0)),
                  pl.BlockSpec((k, tn), lambda i, j: (0, j))],
        out_specs=pl.BlockSpec((tm, tn), lambda i, j: (i, j)),
        out_shape=jax.ShapeDtypeStruct((m, n), F32),
        compiler_params=_cparams(("parallel", "arbitrary")),
        name=name,
    )(x, w)


def _rwkv_prep_kernel(*refs, seq_rows, pad, tm, dr, n_heads, d_dec, d_icl, sample):
    if sample:
        (rkv_ref, lora_ref, sp_rkv_ref, sp_lora_ref, mu_rkv_ref, mu_lora_ref, wd_ref, wa_ref, wg_ref,
         w0_ref, a0_ref, kk_s_ref, ka_s_ref, rk_ref,
         r_o, w_o, k_o, v_o, kk_o, b_o, g_o, bon_o) = refs
        carry_rkv = carry_lora = None
    else:
        (rkv_ref, lora_ref, mu_rkv_ref, mu_lora_ref, wd_ref, wa_ref, wg_ref,
         w0_ref, a0_ref, kk_s_ref, ka_s_ref, rk_ref,
         r_o, w_o, k_o, v_o, kk_o, b_o, g_o, bon_o, carry_rkv, carry_lora) = refs
    i = pl.program_id(0)
    row = lax.broadcasted_iota(jnp.int32, (tm, 1), 0)
    p_rkv = rkv_ref[...]
    p_lora = lora_ref[...]
    if sample:
        first = _imod(row, seq_rows) == 0
        prev_rkv = jnp.where(first, sp_rkv_ref[...], pltpu.roll(p_rkv, 1, 0))
        prev_lora = jnp.where(first, sp_lora_ref[...], pltpu.roll(p_lora, 1, 0))
    else:
        @pl.when(i == 0)
        def _():
            carry_rkv[...] = jnp.zeros_like(carry_rkv)
            carry_lora[...] = jnp.zeros_like(carry_lora)

        pos = (i * tm) % seq_rows + row
        real = pos >= pad
        p_rkv = jnp.where(real, p_rkv, 0.0)
        p_lora = jnp.where(real, p_lora, 0.0)
        prev_rkv = jnp.where(row == 0, carry_rkv[0:1, :], pltpu.roll(p_rkv, 1, 0))
        prev_lora = jnp.where(row == 0, carry_lora[0:1, :], pltpu.roll(p_lora, 1, 0))
        prev_rkv = jnp.where(pos > pad, prev_rkv, 0.0)
        prev_lora = jnp.where(pos > pad, prev_lora, 0.0)
        carry_rkv[0:1, :] = p_rkv[tm - 1:tm, :]
        carry_lora[0:1, :] = p_lora[tm - 1:tm, :]

    xs_rkv = p_rkv + (prev_rkv - p_rkv) * mu_rkv_ref[...]
    xl = p_lora + (prev_lora - p_lora) * mu_lora_ref[...]
    r = xs_rkv[:, 0:dr]
    k = xs_rkv[:, dr:2 * dr]
    v = xs_rkv[:, 2 * dr:3 * dr]

    dec = _dot(jnp.tanh(xl).astype(BF16), wd_ref[...])
    a_pre = _dot(xl.astype(BF16), wa_ref[...])
    g = _dot(_sigmoid(xl).astype(BF16), wg_ref[...])
    w_log = -_softplus(-(w0_ref[...] + dec)) - 0.5
    decay = jnp.exp(-jnp.exp(w_log))
    a = _sigmoid(a0_ref[...] + a_pre)

    sel = _head_select(dr, n_heads)
    exp = _head_expand(n_heads, dr)
    kk = k * kk_s_ref[...]
    nrm = jnp.sqrt(_head_sum_bcast(kk * kk, sel, exp))
    kk = kk / jnp.maximum(nrm, 1e-12)
    k_mod = k * (1.0 + (a - 1.0) * ka_s_ref[...])
    bonus = _head_sum_bcast(r * k_mod * rk_ref[...], sel, exp) * v

    r_o[...] = r
    w_o[...] = decay
    k_o[...] = k_mod
    v_o[...] = v
    kk_o[...] = kk
    b_o[...] = kk * a
    g_o[...] = g
    bon_o[...] = bonus


def _rwkv_prep(p, sp_rkv, sp_lora, prm, *, seq_rows, pad, sample):
    m = p.shape[0]
    dr = prm["dr"]
    dl = prm["lora_w"]
    tm = _pick(m if sample else seq_rows, (128, 64, 32, 16, 8))
    assert tm % seq_rows == 0 or not sample
    lora_blk = (3 * dr + 3 * prm["df"]) // dl
    row_rkv = pl.BlockSpec((tm, 3 * dr), lambda i: (i, 0))
    row_lora = pl.BlockSpec((tm, dl), lambda i: (i, lora_blk))
    full = lambda a: pl.BlockSpec(a.shape, lambda i: (0,) * a.ndim)
    consts = [prm["mu_rkv"], prm["mu_lora"], prm["wd"], prm["wa"], prm["wg"], prm["w0"], prm["a0"],
              prm["k_k"], prm["k_a"], prm["r_k"]]
    in_specs = [row_rkv, row_lora]
    args = [p, p]
    if sample:
        in_specs += [pl.BlockSpec((tm, 3 * dr), lambda i: (i, 0)), pl.BlockSpec((tm, dl), lambda i: (i, 0))]
        args += [sp_rkv, sp_lora]
    in_specs += [full(c) for c in consts]
    args += consts
    out_spec = pl.BlockSpec((tm, dr), lambda i: (i, 0))
    scratch = [] if sample else [pltpu.VMEM((8, 3 * dr), F32), pltpu.VMEM((8, dl), F32)]
    return pl.pallas_call(
        functools.partial(_rwkv_prep_kernel, seq_rows=seq_rows, pad=pad, tm=tm, dr=dr,
                          n_heads=dr // HEAD_DIM, d_dec=prm["d_dec"], d_icl=prm["d_icl"], sample=sample),
        grid=(m // tm,),
        in_specs=in_specs,
        out_specs=[out_spec] * 8,
        out_shape=[jax.ShapeDtypeStruct((m, dr), F32)] * 8,
        scratch_shapes=scratch,
        compiler_params=_cparams(("arbitrary",)),
        name="rwkv_prep_sample" if sample else "rwkv_prep_prompt",
    )(*args)


CHAINS = LANES // 2
KH = HEAD_DIM // 2


def _rwkv_scan_kernel(r_ref, w_ref, k_ref, kk_ref, b_ref, v_ref, s0_ref, o_ref, sf_ref, s_scr, *, tb):
    t_blk = pl.program_id(1)

    @pl.when(t_blk == 0)
    def _():
        s_scr[...] = s0_ref[...]

    def step(t, carry):
        r = r_ref[t]
        w = w_ref[t]
        k = k_ref[t]
        kk = kk_ref[t]
        b = b_ref[t]
        for vi in range(HEAD_DIM):
            sv = s_scr[vi]
            skk = jnp.sum(sv * kk, axis=0, keepdims=True)
            skk = skk + pltpu.roll(skk, CHAINS, 1)
            vrow = v_ref[t, pl.ds(vi, 1), :]
            sv = sv * w - skk * b + vrow * k
            s_scr[vi] = sv
            o = jnp.sum(sv * r, axis=0, keepdims=True)
            o_ref[t, pl.ds(vi, 1), :] = o + pltpu.roll(o, CHAINS, 1)
        return carry

    lax.fori_loop(0, tb, step, 0)

    @pl.when(t_blk == pl.num_programs(1) - 1)
    def _():
        sf_ref[...] = s_scr[...]


def _to_chain_layout(x, n_seq, t):
    h = x.shape[1] // HEAD_DIM
    g = (n_seq * h) // CHAINS
    x = x.reshape(n_seq, t, h, 2, KH).transpose(1, 4, 0, 2, 3)
    x = x.reshape(t, KH, g, CHAINS, 2).transpose(0, 1, 2, 4, 3)
    return x.reshape(t, KH, g * LANES)


def _value_to_chain_layout(x, n_seq, t):
    h = x.shape[1] // HEAD_DIM
    g = (n_seq * h) // CHAINS
    x = x.reshape(n_seq, t, h, HEAD_DIM).transpose(1, 3, 0, 2).reshape(t, HEAD_DIM, g, 1, CHAINS)
    return jnp.broadcast_to(x, (t, HEAD_DIM, g, 2, CHAINS)).reshape(t, HEAD_DIM, g * LANES)


def _value_from_chain_layout(o, n_seq, t, h):
    g = (n_seq * h) // CHAINS
    o = o.reshape(t, HEAD_DIM, g, 2, CHAINS)[:, :, :, 0, :].reshape(t, HEAD_DIM, n_seq, h)
    return o.transpose(2, 0, 3, 1).reshape(n_seq * t, h * HEAD_DIM)


def _state_to_chain_layout(s):
    n_seq, h = s.shape[:2]
    g = (n_seq * h) // CHAINS
    s = s.reshape(g, CHAINS, HEAD_DIM, 2, KH).transpose(2, 4, 0, 3, 1)
    return s.reshape(HEAD_DIM, KH, g * LANES)


def _state_from_chain_layout(s, n_seq, h):
    g = (n_seq * h) // CHAINS
    s = s.reshape(HEAD_DIM, KH, g, 2, CHAINS).transpose(2, 4, 0, 3, 1)
    return s.reshape(n_seq, h, HEAD_DIM, HEAD_DIM)


def _rwkv_scan(r, w, k, kk, b, v, s0, n_seq, t):
    h = r.shape[1] // HEAD_DIM
    g = (n_seq * h) // CHAINS
    tb = _pick(t, (64, 32, 16, 8, 4, 2, 1))
    ins = [_to_chain_layout(a, n_seq, t) for a in (r, w, k, kk, b)]
    vin = _value_to_chain_layout(v, n_seq, t)
    sin = _state_to_chain_layout(s0)
    kspec = pl.BlockSpec((tb, KH, LANES), lambda gi, ti: (ti, 0, gi))
    vspec = pl.BlockSpec((tb, HEAD_DIM, LANES), lambda gi, ti: (ti, 0, gi))
    sspec = pl.BlockSpec((HEAD_DIM, KH, LANES), lambda gi, ti: (0, 0, gi))
    o, sf = pl.pallas_call(
        functools.partial(_rwkv_scan_kernel, tb=tb),
        grid=(g, t // tb),
        in_specs=[kspec] * 5 + [vspec, sspec],
        out_specs=[vspec, sspec],
        out_shape=[jax.ShapeDtypeStruct((t, HEAD_DIM, g * LANES), F32),
                   jax.ShapeDtypeStruct((HEAD_DIM, KH, g * LANES), F32)],
        scratch_shapes=[pltpu.VMEM((HEAD_DIM, KH, LANES), F32)],
        compiler_params=_cparams(("parallel", "arbitrary")),
        name="rwkv_scan",
    )(*ins, vin, sin)
    return _value_from_chain_layout(o, n_seq, t, h), _state_from_chain_layout(sf, n_seq, h)


def _rwkv_post_kernel(o_ref, g_ref, bon_ref, gg_ref, gb_ref, out_ref, *, dr):
    n_heads = dr // HEAD_DIM
    sel = _head_select(dr, n_heads)
    exp = _head_expand(n_heads, dr)
    o = o_ref[...]
    mu = _head_sum_bcast(o, sel, exp) * (1.0 / HEAD_DIM)
    oc = o - mu
    var = _head_sum_bcast(oc * oc, sel, exp) * (1.0 / HEAD_DIM)
    y = oc * lax.rsqrt(var + GN_EPS) * gg_ref[...] + gb_ref[...]
    out_ref[...] = ((y + bon_ref[...]) * g_ref[...]).astype(BF16)


def _rwkv_post(o, g, bonus, gn_g, gn_b):
    m, dr = o.shape
    tm = _pick(m, (256, 128, 64, 32, 16, 8))
    row = pl.BlockSpec((tm, dr), lambda i: (i, 0))
    vec = pl.BlockSpec((1, dr), lambda i: (0, 0))
    return pl.pallas_call(
        functools.partial(_rwkv_post_kernel, dr=dr),
        grid=(m // tm,),
        in_specs=[row, row, row, vec, vec],
        out_specs=row,
        out_shape=jax.ShapeDtypeStruct((m, dr), BF16),
        compiler_params=_cparams(("parallel",)),
        name="rwkv_post",
    )(o, g, bonus, gn_g.reshape(1, dr), gn_b.reshape(1, dr))


def _logf_kernel(lora_ref, bf_ref, lf_ref, c_ref, carry, *, tb, pad, f_off, n_heads):
    i = pl.program_id(1)

    @pl.when(i == 0)
    def _():
        carry[...] = jnp.zeros_like(carry)

    z = lora_ref[:, f_off:f_off + n_heads] + bf_ref[...]
    lf = -_softplus(-z)
    pos = i * tb + lax.broadcasted_iota(jnp.int32, (tb, 1), 0)
    lf = jnp.where(pos >= pad, lf, 0.0)
    rr = lax.broadcasted_iota(jnp.int32, (tb, tb), 0)
    cc = lax.broadcasted_iota(jnp.int32, (tb, tb), 1)
    tril = jnp.where(cc <= rr, 1.0, 0.0).astype(BF16)
    c = _dot_01x(tril, lf) + carry[0:1, :]
    lf_ref[...] = lf
    c_ref[...] = c
    carry[0:1, :] = c[tb - 1:tb, :]


def _logf_cumsum(p, b_f, prm, n_seq, seq_rows, pad):
    dl = prm["lora_w"]
    hf = b_f.shape[-1]
    tb = _pick(seq_rows, (128, 64, 32, 16, 8))
    lora_blk = (3 * prm["dr"] + 3 * prm["df"]) // dl
    p3 = p.reshape(n_seq, seq_rows, p.shape[-1])
    out = pl.BlockSpec((None, tb, hf), lambda s, i: (s, i, 0))
    return pl.pallas_call(
        functools.partial(_logf_kernel, tb=tb, pad=pad, f_off=prm["f_off"], n_heads=hf),
        grid=(n_seq, seq_rows // tb),
        in_specs=[pl.BlockSpec((None, tb, dl), lambda s, i: (s, i, lora_blk)),
                  pl.BlockSpec((1, hf), lambda s, i: (0, 0))],
        out_specs=[out, out],
        out_shape=[jax.ShapeDtypeStruct((n_seq, seq_rows, hf), F32)] * 2,
        scratch_shapes=[pltpu.VMEM((8, hf), F32)],
        compiler_params=_cparams(("parallel", "arbitrary")),
        name="logf_cumsum",
    )(p3, b_f.reshape(1, hf))


def _fox_prompt_kernel(q_ref, k_ref, v_ref, cq_ref, ck_ref, o_ref, m_scr, l_scr, acc_scr, *, tq, pad, scale):
    i = pl.program_id(2)
    j = pl.program_id(3)

    @pl.when(j == 0)
    def _():
        m_scr[...] = jnp.full_like(m_scr, NEG_INF)
        l_scr[...] = jnp.zeros_like(l_scr)
        acc_scr[...] = jnp.zeros_like(acc_scr)

    lane = lax.broadcasted_iota(jnp.int32, (tq, LANES), 1)

    @pl.when(j <= i)
    def _():
        q = q_ref[...] * scale
        kb = k_ref[...].astype(BF16)
        vb = v_ref[...].astype(BF16)
        qpos = i * tq + lax.broadcasted_iota(jnp.int32, (tq, tq), 0)
        kpos = j * tq + lax.broadcasted_iota(jnp.int32, (tq, tq), 1)
        valid = (kpos <= qpos) & (kpos >= pad)
        for h in range(2):
            head_lanes = (lane < HEAD_DIM) if h == 0 else (lane >= HEAD_DIM)
            qh = jnp.where(head_lanes, q, 0.0).astype(BF16)
            s = _dot_nt(qh, kb) + cq_ref[:, h:h + 1] - ck_ref[h:h + 1, :]
            s = jnp.where(valid, s, NEG_INF)
            m_prev = m_scr[h]
            m_new = jnp.maximum(m_prev, jnp.max(s, axis=-1, keepdims=True))
            alpha = jnp.exp(m_prev - m_new)
            p = jnp.exp(s - m_new)
            l_scr[h] = alpha * l_scr[h] + jnp.sum(p, axis=-1, keepdims=True)
            acc_scr[h] = alpha * acc_scr[h] + _dot(p.astype(BF16), vb)
            m_scr[h] = m_new

    @pl.when(j == i)
    def _():
        o = jnp.where(lane < HEAD_DIM, acc_scr[0] / l_scr[0], acc_scr[1] / l_scr[1])
        o_ref[...] = o.astype(BF16)


def _fox_prompt(p, c, prm, n_seq, seq_rows, pad):
    dr, df = prm["dr"], prm["df"]
    hf = df // HEAD_DIM
    n_pairs = hf // 2
    tq = _pick(seq_rows, (384, 256, 128))
    nq = seq_rows // tq
    p3 = p.reshape(n_seq, seq_rows, p.shape[-1])
    c4 = c.reshape(n_seq, seq_rows, n_pairs, 2)
    cq = c4.transpose(0, 2, 1, 3)
    ck = c4.transpose(0, 2, 3, 1)
    q0, k0, v0 = (3 * dr) // LANES, (3 * dr + df) // LANES, (3 * dr + 2 * df) // LANES
    return pl.pallas_call(
        functools.partial(_fox_prompt_kernel, tq=tq, pad=pad, scale=HEAD_DIM ** -0.5),
        grid=(n_seq, n_pairs, nq, nq),
        in_specs=[pl.BlockSpec((None, tq, LANES), lambda b, pr, i, j: (b, i, q0 + pr)),
                  pl.BlockSpec((None, tq, LANES), lambda b, pr, i, j: (b, jnp.minimum(j, i), k0 + pr)),
                  pl.BlockSpec((None, tq, LANES), lambda b, pr, i, j: (b, jnp.minimum(j, i), v0 + pr)),
                  pl.BlockSpec((None, None, tq, 2), lambda b, pr, i, j: (b, pr, i, 0)),
                  pl.BlockSpec((None, None, 2, tq), lambda b, pr, i, j: (b, pr, 0, jnp.minimum(j, i)))],
        out_specs=pl.BlockSpec((None, tq, LANES), lambda b, pr, i, j: (b, i, pr)),
        out_shape=jax.ShapeDtypeStruct((n_seq, seq_rows, df), BF16),
        scratch_shapes=[pltpu.VMEM((2, tq, 1), F32), pltpu.VMEM((2, tq, 1), F32),
                        pltpu.VMEM((2, tq, LANES), F32)],
        compiler_params=_cparams(("parallel", "parallel", "parallel", "arbitrary")),
        name="fox_prompt",
    )(p3, p3, p3, cq, ck).reshape(n_seq * seq_rows, df)


def _cache_bias_kernel(pt_ref, lf_ref, d_ref, carry, *, page):
    p = pl.program_id(1)

    @pl.when(p == 0)
    def _():
        carry[...] = jnp.zeros_like(carry)

    lf = lf_ref[...].astype(F32)
    rr = lax.broadcasted_iota(jnp.int32, (page, page), 0)
    cc = lax.broadcasted_iota(jnp.int32, (page, page), 1)
    upper = jnp.where(cc > rr, 1.0, 0.0).astype(BF16)
    d_ref[...] = _dot_01x(upper, lf) + carry[0:1, :]
    carry[0:1, :] = carry[0:1, :] + jnp.sum(lf, axis=0, keepdims=True)


def _cache_bias(cache_logf3, page_table_flat, n_seq, n_pages, page_off):
    page, hf = cache_logf3.shape[1:]
    grid_spec = pltpu.PrefetchScalarGridSpec(
        num_scalar_prefetch=1,
        grid=(n_seq, n_pages),
        in_specs=[pl.BlockSpec((None, page, hf),
                               lambda b, p, pt: (page_off + pt[b * n_pages + n_pages - 1 - p], 0, 0))],
        out_specs=pl.BlockSpec((None, page, hf), lambda b, p, pt: (b, n_pages - 1 - p, 0)),
        scratch_shapes=[pltpu.VMEM((8, hf), F32)],
    )
    return pl.pallas_call(
        functools.partial(_cache_bias_kernel, page=page),
        grid_spec=grid_spec,
        out_shape=jax.ShapeDtypeStruct((n_seq, n_pages * page, hf), F32),
        compiler_params=_cparams(("parallel", "arbitrary")),
        name="cache_bias",
    )(page_table_flat, cache_logf3)


def _fox_sample_kernel(pt_ref, q_ref, kn_ref, vn_ref, cncol_ref, cnrow_ref, dt_ref, *rest,
                       pp, page, nq, hf, scale):
    k_refs = rest[:pp]
    v_refs = rest[pp:2 * pp]
    o_ref, qbd_scr, m_scr, l_scr, acc_scr = rest[2 * pp:]
    pb = pl.program_id(1)
    rows = hf * nq
    dcols = hf * HEAD_DIM
    own = (_idiv(lax.broadcasted_iota(jnp.int32, (rows, dcols), 0), nq)
           == _idiv(lax.broadcasted_iota(jnp.int32, (rows, dcols), 1), HEAD_DIM))

    @pl.when(pb == 0)
    def _():
        q = q_ref[...] * scale
        qt = jnp.broadcast_to(q[None], (hf, nq, dcols)).reshape(rows, dcols)
        qbd_scr[...] = jnp.where(own, qt, 0.0).astype(BF16)
        m_scr[...] = jnp.full_like(m_scr, NEG_INF)
        l_scr[...] = jnp.zeros_like(l_scr)
        acc_scr[...] = jnp.zeros_like(acc_scr)

    def update(s, vals):
        m_prev = m_scr[...]
        m_new = jnp.maximum(m_prev, jnp.max(s, axis=-1, keepdims=True))
        alpha = jnp.exp(m_prev - m_new)
        p = jnp.exp(s - m_new)
        l_scr[...] = alpha * l_scr[...] + jnp.sum(p, axis=-1, keepdims=True)
        acc = alpha * acc_scr[...]
        w = s.shape[1] // len(vals)
        for x, vx in enumerate(vals):
            acc = acc + _dot(p[:, x * w:(x + 1) * w].astype(BF16), vx)
        acc_scr[...] = acc
        m_scr[...] = m_new

    qbd = qbd_scr[...]
    s = jnp.concatenate([_dot_nt(qbd, k_refs[x][...].astype(BF16)) for x in range(pp)], axis=1)
    dexp = jnp.broadcast_to(dt_ref[...][:, None, :], (hf, nq, pp * page)).reshape(rows, pp * page)
    s = s + cncol_ref[...] + dexp
    update(s, [v_refs[x][...].astype(BF16) for x in range(pp)])

    @pl.when(pb == pl.num_programs(1) - 1)
    def _():
        sn = _dot_nt(qbd, kn_ref[...].astype(BF16)) + cnrow_ref[...]
        qi = _imod(lax.broadcasted_iota(jnp.int32, (rows, nq), 0), nq)
        ki = lax.broadcasted_iota(jnp.int32, (rows, nq), 1)
        sn = jnp.where(ki <= qi, sn, NEG_INF)
        update(sn, [vn_ref[...].astype(BF16)])
        o = jnp.where(own, acc_scr[...] / l_scr[...], 0.0)
        o_ref[...] = jnp.sum(o.reshape(hf, nq, dcols), axis=0).astype(BF16)


def _fox_sample(p_s, c_new, dbias_t, cache_k3, cache_v3, page_table_flat, prm, n_seq, nq, n_pages, page_off):
    dr, df = prm["dr"], prm["df"]
    hf = df // HEAD_DIM
    page = cache_k3.shape[1]
    pp = _pick(n_pages, (4, 2, 1))
    npb = n_pages // pp
    rows = hf * nq
    p3 = p_s.reshape(n_seq, nq, p_s.shape[-1])
    cn_t = c_new.transpose(0, 2, 1)
    cn_col = cn_t.reshape(n_seq, rows, 1)
    cn_row = cn_col - jnp.repeat(cn_t, nq, axis=1)
    qb, kb, vb = (3 * dr) // df, (3 * dr + df) // df, (3 * dr + 2 * df) // df
    seq_blk = lambda blk: pl.BlockSpec((None, nq, df), lambda b, i, pt: (b, 0, blk))

    def page_spec(x):
        return pl.BlockSpec((None, page, df),
                            lambda b, i, pt: (page_off + pt[b * n_pages + i * pp + x], 0, 0))

    grid_spec = pltpu.PrefetchScalarGridSpec(
        num_scalar_prefetch=1,
        grid=(n_seq, npb),
        in_specs=[seq_blk(qb), seq_blk(kb), seq_blk(vb),
                  pl.BlockSpec((None, rows, 1), lambda b, i, pt: (b, 0, 0)),
                  pl.BlockSpec((None, rows, nq), lambda b, i, pt: (b, 0, 0)),
                  pl.BlockSpec((None, hf, pp * page), lambda b, i, pt: (b, 0, i))]
                 + [page_spec(x) for x in range(pp)] * 2,
        out_specs=pl.BlockSpec((None, nq, df), lambda b, i, pt: (b, 0, 0)),
        scratch_shapes=[pltpu.VMEM((rows, df), BF16), pltpu.VMEM((rows, 1), F32),
                        pltpu.VMEM((rows, 1), F32), pltpu.VMEM((rows, df), F32)],
    )
    out = pl.pallas_call(
        functools.partial(_fox_sample_kernel, pp=pp, page=page, nq=nq, hf=hf, scale=HEAD_DIM ** -0.5),
        grid_spec=grid_spec,
        out_shape=jax.ShapeDtypeStruct((n_seq, nq, df), BF16),
        compiler_params=_cparams(("parallel", "arbitrary")),
        name="fox_sample",
    )(page_table_flat, p3, p3, p3, cn_col, cn_row, dbias_t, *([cache_k3] * pp), *([cache_v3] * pp))
    return out.reshape(n_seq * nq, df)


def _gelu_tanh(x):
    return 0.5 * x * (1.0 + jnp.tanh(0.7978845608028654 * (x + 0.044715 * (x * x * x))))


def _ffn_in_kernel(*refs, tm, tn, seq_rows, pad, sample):
    if sample:
        x_ref, wu_ref, wg_ref, cw_ref, cb_ref, p1_ref, p2_ref, h_ref, u_ref = refs
    else:
        x_ref, wu_ref, wg_ref, cw_ref, cb_ref, h_ref, tail_ref, carry = refs
    i = pl.program_id(0)
    j = pl.program_id(1)
    x = x_ref[...]
    u = _dot(x, wu_ref[...])
    gate = _dot(x, wg_ref[...])
    row = lax.broadcasted_iota(jnp.int32, (tm, 1), 0)
    if sample:
        t = _imod(row, seq_rows)
        u1 = jnp.where(t == 0, p1_ref[...], pltpu.roll(u, 1, 0))
        u2 = jnp.where(t < 2, p2_ref[...], pltpu.roll(u, 2, 0))
        u_ref[...] = u
    else:
        @pl.when(i == 0)
        def _():
            carry[:, pl.ds(pl.multiple_of(j * tn, tn), tn)] = jnp.zeros((8, tn), F32)

        pos = (i * tm) % seq_rows + row
        u = jnp.where(pos >= pad, u, 0.0)
        prev = carry[:, pl.ds(pl.multiple_of(j * tn, tn), tn)]
        u1 = jnp.where(row == 0, prev[7:8, :], pltpu.roll(u, 1, 0))
        u2 = jnp.where(row == 0, prev[6:7, :], jnp.where(row == 1, prev[7:8, :], pltpu.roll(u, 2, 0)))
        tail = u[tm - 8:tm, :]
        carry[:, pl.ds(pl.multiple_of(j * tn, tn), tn)] = tail
        tail_ref[...] = tail
    c = cb_ref[...] + cw_ref[0:1, :] * u2 + cw_ref[1:2, :] * u1 + cw_ref[2:3, :] * u
    h_ref[...] = (_gelu_tanh(c) * gate).astype(BF16)


def _ffn_in(x1b, w_ffn_in_b, conv_w, conv_b, p1, p2, *, seq_rows, pad, sample):
    m, d = x1b.shape
    dff = conv_w.shape[-1]
    tn = _pick(dff, (256, 128))
    nj = dff // tn
    tm = m if sample else _pick(seq_rows, (528, 512, 384, 256, 128, 64, 32, 16, 8))
    ni = m // tm
    in_specs = [pl.BlockSpec((tm, d), lambda i, j: (i, 0)),
                pl.BlockSpec((d, tn), lambda i, j: (0, j)),
                pl.BlockSpec((d, tn), lambda i, j: (0, j + nj)),
                pl.BlockSpec((3, tn), lambda i, j: (0, j)),
                pl.BlockSpec((1, tn), lambda i, j: (0, j))]
    args = [x1b, w_ffn_in_b, w_ffn_in_b, conv_w, conv_b.reshape(1, dff)]
    tile = pl.BlockSpec((tm, tn), lambda i, j: (i, j))
    if sample:
        in_specs += [tile, tile]
        args += [p1, p2]
        out_specs = [tile, tile]
        out_shape = [jax.ShapeDtypeStruct((m, dff), BF16), jax.ShapeDtypeStruct((m, dff), F32)]
        scratch = []
    else:
        out_specs = [tile, pl.BlockSpec((8, tn), lambda i, j: (i, j))]
        out_shape = [jax.ShapeDtypeStruct((m, dff), BF16), jax.ShapeDtypeStruct((ni * 8, dff), F32)]
        scratch = [pltpu.VMEM((8, dff), F32)]
    return pl.pallas_call(
        functools.partial(_ffn_in_kernel, tm=tm, tn=tn, seq_rows=seq_rows, pad=pad, sample=sample),
        grid=(ni, nj),
        in_specs=in_specs,
        out_specs=out_specs,
        out_shape=out_shape,
        scratch_shapes=scratch,
        compiler_params=_cparams(("arbitrary", "arbitrary")),
        name="ffn_in_sample" if sample else "ffn_in_prompt",
    )(*args)


def _layer_params(lw, d_model):
    (w_in, b_f, mu_shift, w0, w_decay_up, a0, w_iclr_up, w_gate_up, k_k, k_a, r_k,
     gn_g, gn_b, w_out, ln1_g, ln1_b, w_ffn_in, conv_w, conv_b, w_ffn_out, ln2_g, ln2_b) = lw
    dr = w0.shape[-1]
    hf = b_f.shape[-1]
    df = hf * HEAD_DIM
    d_dec, d_icl, d_gate = w_decay_up.shape[0], w_iclr_up.shape[0], w_gate_up.shape[0]
    rw_cols = 3 * dr + d_dec + d_icl + d_gate
    n_lora = d_dec + d_icl + d_gate
    lora_w = -(-(n_lora + hf) // (4 * LANES)) * (4 * LANES)
    assert (3 * dr + 3 * df) % lora_w == 0 and dr % LANES == 0 and df % LANES == 0
    fox0 = rw_cols
    w_perm = jnp.concatenate(
        [w_in[:, :3 * dr], w_in[:, fox0:fox0 + 3 * df], w_in[:, 3 * dr:rw_cols],
         w_in[:, fox0 + 3 * df:], jnp.zeros((d_model, lora_w - n_lora - hf), w_in.dtype)], axis=1).astype(BF16)

    def pad_rows(w, off):
        return jnp.zeros((lora_w, dr), F32).at[off:off + w.shape[0]].set(w).astype(BF16)

    prm = dict(
        dr=dr, df=df, lora_w=lora_w, d_dec=d_dec, d_icl=d_icl, f_off=n_lora, rw_cols=rw_cols,
        mu_rkv=mu_shift[:3 * dr].reshape(1, -1),
        mu_lora=jnp.zeros((1, lora_w), F32).at[0, :n_lora].set(mu_shift[3 * dr:]),
        wd=pad_rows(w_decay_up, 0), wa=pad_rows(w_iclr_up, d_dec), wg=pad_rows(w_gate_up, d_dec + d_icl),
        w0=w0.reshape(1, dr), a0=a0.reshape(1, dr), k_k=k_k.reshape(1, dr), k_a=k_a.reshape(1, dr),
        r_k=r_k.reshape(1, dr), w_perm=w_perm, w_out=w_out.astype(BF16), w_ffn_in=w_ffn_in.astype(BF16),
        w_ffn_out=w_ffn_out.astype(BF16), b_f=b_f, gn_g=gn_g, gn_b=gn_b, ln1=(ln1_g, ln1_b),
        ln2=(ln2_g, ln2_b), conv_w=conv_w, conv_b=conv_b)
    return prm


def _mm_tiles(m, n, seq_rows):
    tm = _pick(seq_rows, (1056, 1024, 768, 512, 384, 256, 128, 64, 32, 16, 8)) if m > 512 else m
    tn = _pick(n, (640, 512, 256, 128))
    return tm, tn


def _permute_rw_row(row, prm):
    dr = prm["dr"]
    n_lora = prm["f_off"]
    lora = jnp.zeros(row.shape[:-1] + (prm["lora_w"],), F32).at[..., :n_lora].set(row[..., 3 * dr:])
    return row[..., :3 * dr], lora


def _run_group(x2, ln0, prm, alpha, *, n_seq, seq_rows, pad, skip, sample, shift_prev, s0, ffn_prev, attend):
    ln0_g, ln0_b = ln0
    m, d = x2.shape
    dr, df = prm["dr"], prm["df"]
    h_b = _ln0(x2, ln0_g, ln0_b)
    tm, tn = _mm_tiles(m, prm["w_perm"].shape[1], seq_rows)
    p = _matmul(h_b, prm["w_perm"], tm, tn, "in_proj")

    if sample:
        sp_rkv, sp_lora = _permute_rw_row(shift_prev, prm)
        expand = lambda a: jnp.zeros((n_seq, seq_rows, a.shape[-1]), F32).at[:, 0].set(a).reshape(m, -1)
        sp_rkv, sp_lora = expand(sp_rkv), expand(sp_lora)
    else:
        sp_rkv = sp_lora = None
    r, w, k, v, kk, b, g, bonus = _rwkv_prep(p, sp_rkv, sp_lora, prm, seq_rows=seq_rows, pad=pad, sample=sample)
    o_scan, s_last = _rwkv_scan(r, w, k, kk, b, v, s0, n_seq, seq_rows)
    o_rw = _rwkv_post(o_scan, g, bonus, prm["gn_g"], prm["gn_b"])

    lf, c = _logf_cumsum(p, prm["b_f"], prm, n_seq, seq_rows, pad)
    o_fox = attend(p, c)

    mix_in = jnp.concatenate([o_rw, o_fox], axis=1)
    tm, tn = _mm_tiles(m, d, seq_rows)
    mix = _matmul(mix_in, prm["w_out"], tm, tn, "out_proj")
    x1, x1b = _ln1(x2, mix, ln0_g, ln0_b, *prm["ln1"], alpha)

    if sample:
        p1 = jnp.zeros((n_seq, seq_rows, ffn_prev.shape[-1]), F32).at[:, 0].set(ffn_prev[:, 1])
        p2 = p1.at[:, 0].set(ffn_prev[:, 0]).at[:, 1].set(ffn_prev[:, 1])
        p1, p2 = p1.reshape(m, -1), p2.reshape(m, -1)
    else:
        p1 = p2 = None
    hid, u_aux = _ffn_in(x1b, prm["w_ffn_in"], prm["conv_w"], prm["conv_b"], p1, p2,
                         seq_rows=seq_rows, pad=pad, sample=sample)
    dff = hid.shape[1]
    tm_o = _pick(seq_rows, (528, 512, 384, 256, 128, 64, 32, 16, 8)) if m > 512 else m
    ffn = _matmul(hid, prm["w_ffn_out"], tm_o, _pick(d, (256, 128)), "ffn_out")
    y = _ln2(x1, ffn, *prm["ln2"], alpha, n_seq, seq_rows, skip)

    p3 = p.reshape(n_seq, seq_rows, -1)
    hf = df // HEAD_DIM
    k_new = p3[:, pad:, 3 * dr + df:3 * dr + 2 * df].reshape(n_seq, seq_rows - pad, hf, HEAD_DIM)
    v_new = p3[:, pad:, 3 * dr + 2 * df:3 * dr + 3 * df].reshape(n_seq, seq_rows - pad, hf, HEAD_DIM)
    last = p3[:, -1]
    shift_last = jnp.concatenate([last[:, :3 * dr], last[:, 3 * dr + 3 * df:3 * dr + 3 * df + prm["f_off"]]], axis=1)
    if sample:
        conv_state = u_aux.reshape(n_seq, seq_rows, dff)[:, -2:]
    else:
        tiles_per_seq = u_aux.shape[0] // 8 // n_seq
        conv_state = u_aux.reshape(n_seq, tiles_per_seq, 8, dff)[:, -1, -2:]
    return y, (k_new, v_new, lf[:, pad:], s_last, shift_last, conv_state)


def kernel(x_prompt, x_sample, cache_k, cache_v, cache_logf, state_rwkv, state_shift, state_ffn_conv,
           page_table, meta_tokens, ln0_g, ln0_b, w_in, b_f, mu_shift, w0, w_decay_up, a0, w_iclr_up,
           w_gate_up, k_k, k_a, r_k, gn_g, gn_b, w_out, ln1_g, ln1_b, w_ffn_in, conv_w, conv_b,
           w_ffn_out, ln2_g, ln2_b):
    depth = w_in.shape[0]
    assert depth == 1, "the token layout between layers is only wired for a single layer"
    b, seq, d = x_prompt.shape
    db, nq, _ = x_sample.shape
    n_meta = meta_tokens.shape[0]
    alpha = (2 * depth) ** 0.25
    n_pool, page = cache_k.shape[1:3]
    n_pages = page_table.shape[1]
    h_rw = r_k.shape[1]

    t_real = seq + n_meta
    pad = (-t_real) % Q_BLOCK
    t_pad = t_real + pad
    meta = jnp.broadcast_to(meta_tokens[None].astype(x_prompt.dtype), (b, n_meta, d))
    xp = jnp.concatenate([jnp.zeros((b, pad, d), x_prompt.dtype), meta, x_prompt], axis=1).reshape(b * t_pad, d)
    xs = x_sample.reshape(db * nq, d)
    pt_flat = page_table.reshape(-1).astype(jnp.int32)

    outs_p, outs_s = [], []
    for l in range(depth):
        lw = tuple(wt[l] for wt in (w_in, b_f, mu_shift, w0, w_decay_up, a0, w_iclr_up, w_gate_up, k_k, k_a,
                                     r_k, gn_g, gn_b, w_out, ln1_g, ln1_b, w_ffn_in, conv_w, conv_b,
                                     w_ffn_out, ln2_g, ln2_b))
        prm = _layer_params(lw, d)
        hf = prm["df"] // HEAD_DIM

        attend_p = lambda p, c: _fox_prompt(p, c, prm, b, t_pad, pad)
        y_p, st_p = _run_group(xp, (ln0_g, ln0_b), prm, alpha, n_seq=b, seq_rows=t_pad, pad=pad, skip=pad + n_meta, sample=False,
                               shift_prev=None, s0=jnp.zeros((b, h_rw, HEAD_DIM, HEAD_DIM), F32),
                               ffn_prev=None, attend=attend_p)

        ck3 = cache_k.reshape(depth * n_pool, page, hf * HEAD_DIM)
        cv3 = cache_v.reshape(depth * n_pool, page, hf * HEAD_DIM)
        cl3 = cache_logf.reshape(depth * n_pool, page, hf)
        dbias = _cache_bias(cl3, pt_flat, db, n_pages, l * n_pool)
        dbias_t = dbias.transpose(0, 2, 1)
        attend_s = lambda p, c: _fox_sample(p, c, dbias_t, ck3, cv3, pt_flat, prm, db, nq, n_pages, l * n_pool)
        y_s, st_s = _run_group(xs, (ln0_g, ln0_b), prm, alpha, n_seq=db, seq_rows=nq, pad=0, skip=0, sample=True,
                               shift_prev=state_shift[l], s0=state_rwkv[l], ffn_prev=state_ffn_conv[l],
                               attend=attend_s)
        outs_p.append(st_p)
        outs_s.append(st_s)

    k_p, v_p, lf_p, rw_p, sh_p, cv_p = (jnp.stack([o[i] for o in outs_p]) for i in range(6))
    k_s, v_s, lf_s, rw_s, sh_s, cv_s = (jnp.stack([o[i] for o in outs_s]) for i in range(6))
    y_prompt = y_p
    y_sample = y_s.reshape(db, nq, d)
    return (y_prompt, y_sample, k_p, v_p, lf_p, rw_p, sh_p, cv_p, k_s, v_s, lf_s, rw_s, sh_s, cv_s)
```

```python
import functools

import jax
import jax.numpy as jnp
from jax import lax
from jax.experimental import pallas as pl
from jax.experimental.pallas import tpu as pltpu

F32 = jnp.float32
BF16 = jnp.bfloat16

HEAD_DIM = 64
LANES = 128
SUBLANES = 8
Q_BLOCK = 128
LN_EPS = 1e-5
GN_EPS = 64e-5
NEG_INF = -1e30
VMEM_LIMIT = 56 * 1024 * 1024


def _cparams(sem):
    return pltpu.CompilerParams(dimension_semantics=sem, vmem_limit_bytes=VMEM_LIMIT)


def _pick(n, candidates):
    for c in candidates:
        if n % c == 0:
            return c
    return n


def _split_bf16(x):
    hi = x.astype(BF16)
    lo = (x - hi.astype(F32)).astype(BF16)
    return hi, lo


def _dot(a, b):
    return jnp.dot(a, b, preferred_element_type=F32)


def _dot_nt(a, b):
    return lax.dot_general(a, b, (((1,), (1,)), ((), ())), preferred_element_type=F32)


def _dot_x01(x, m01):
    hi, lo = _split_bf16(x)
    return _dot(hi, m01) + _dot(lo, m01)


def _dot_01x(m01, x):
    hi, lo = _split_bf16(x)
    return _dot(m01, hi) + _dot(m01, lo)


def _idiv(x, c):
    return lax.shift_right_logical(x, c.bit_length() - 1) if c & (c - 1) == 0 else x // c


def _imod(x, c):
    return (x & (c - 1)) if c & (c - 1) == 0 else x % c


def _head_select(n_cols, n_heads):
    c = lax.broadcasted_iota(jnp.int32, (n_cols, n_heads), 0)
    h = lax.broadcasted_iota(jnp.int32, (n_cols, n_heads), 1)
    return jnp.where(_idiv(c, HEAD_DIM) == h, 1.0, 0.0).astype(BF16)


def _head_expand(n_heads, n_cols):
    h = lax.broadcasted_iota(jnp.int32, (n_heads, n_cols), 0)
    c = lax.broadcasted_iota(jnp.int32, (n_heads, n_cols), 1)
    return jnp.where(_idiv(c, HEAD_DIM) == h, 1.0, 0.0).astype(BF16)


def _head_sum_bcast(x, sel, exp):
    return _dot_x01(_dot_x01(x, sel), exp)


def _softplus(z):
    return jnp.maximum(z, 0.0) + jnp.log1p(jnp.exp(-jnp.abs(z)))


def _sigmoid(z):
    return 1.0 / (1.0 + jnp.exp(-z))


def _ln_rows(x, g, b):
    mu = jnp.mean(x, axis=-1, keepdims=True)
    xc = x - mu
    var = jnp.mean(xc * xc, axis=-1, keepdims=True)
    return xc * lax.rsqrt(var + LN_EPS) * g + b


def _ln0_kernel(x_ref, g_ref, b_ref, o_ref):
    o_ref[...] = _ln_rows(x_ref[...], g_ref[...], b_ref[...]).astype(o_ref.dtype)


def _ln0(x2, g, b):
    m, d = x2.shape
    tm = _pick(m, (528, 512, 384, 256, 128, 64, 32, 16, 8))
    return pl.pallas_call(
        _ln0_kernel,
        grid=(m // tm,),
        in_specs=[pl.BlockSpec((tm, d), lambda i: (i, 0)),
                  pl.BlockSpec((1, d), lambda i: (0, 0)),
                  pl.BlockSpec((1, d), lambda i: (0, 0))],
        out_specs=pl.BlockSpec((tm, d), lambda i: (i, 0)),
        out_shape=jax.ShapeDtypeStruct((m, d), BF16),
        compiler_params=_cparams(("parallel",)),
        name="ln0",
    )(x2, g.reshape(1, d), b.reshape(1, d))


def _ln1_kernel(x_ref, mix_ref, g0_ref, b0_ref, g_ref, b_ref, of_ref, ob_ref, *, alpha):
    h = _ln_rows(x_ref[...], g0_ref[...], b0_ref[...])
    x1 = _ln_rows(alpha * h + mix_ref[...], g_ref[...], b_ref[...])
    of_ref[...] = x1
    ob_ref[...] = x1.astype(BF16)


def _ln1(x2, mix, g0, b0, g, b, alpha):
    m, d = x2.shape
    tm = _pick(m, (264, 256, 128, 64, 32, 16, 8))
    row = pl.BlockSpec((tm, d), lambda i: (i, 0))
    vec = pl.BlockSpec((1, d), lambda i: (0, 0))
    return pl.pallas_call(
        functools.partial(_ln1_kernel, alpha=alpha),
        grid=(m // tm,),
        in_specs=[row, row, vec, vec, vec, vec],
        out_specs=[row, row],
        out_shape=[jax.ShapeDtypeStruct((m, d), F32), jax.ShapeDtypeStruct((m, d), BF16)],
        compiler_params=_cparams(("parallel",)),
        name="ln1",
    )(x2, mix, g0.reshape(1, d), b0.reshape(1, d), g.reshape(1, d), b.reshape(1, d))


def _ln2_kernel(x1_ref, ffn_ref, g_ref, b_ref, o_ref, *, alpha):
    o_ref[...] = _ln_rows(alpha * x1_ref[...] + ffn_ref[...], g_ref[...], b_ref[...])


def _ln2(x1, ffn, g, b, alpha, n_seq, seq_rows, skip_rows):
    d = x1.shape[-1]
    out_rows = seq_rows - skip_rows
    tb = next(c for c in (256, 128, 64, 32, 16, 8) if skip_rows % c == 0 and out_rows % c == 0)
    skip = skip_rows // tb
    x3 = x1.reshape(n_seq, seq_rows, d)
    f3 = ffn.reshape(n_seq, seq_rows, d)
    row_in = pl.BlockSpec((None, tb, d), lambda s, i: (s, i + skip, 0))
    vec = pl.BlockSpec((1, d), lambda s, i: (0, 0))
    return pl.pallas_call(
        functools.partial(_ln2_kernel, alpha=alpha),
        grid=(n_seq, out_rows // tb),
        in_specs=[row_in, row_in, vec, vec],
        out_specs=pl.BlockSpec((None, tb, d), lambda s, i: (s, i, 0)),
        out_shape=jax.ShapeDtypeStruct((n_seq, out_rows, d), F32),
        compiler_params=_cparams(("parallel", "parallel")),
        name="ln2",
    )(x3, f3, g.reshape(1, d), b.reshape(1, d))


def _mm_kernel(x_ref, w_ref, o_ref):
    o_ref[...] = _dot(x_ref[...], w_ref[...])


def _matmul(x, w, tm, tn, name):
    m, k = x.shape
    n = w.shape[1]
    return pl.pallas_call(
        _mm_kernel,
        grid=(m // tm, n // tn),
        in_specs=[pl.BlockSpec((tm, k), lambda i, j: (i, 0)),
                  pl.BlockSpec((k, tn), lambda i, j: (0, j))],
        out_specs=pl.BlockSpec((tm, tn), lambda i, j: (i, j)),
        out_shape=jax.ShapeDtypeStruct((m, n), F32),
        compiler_params=_cparams(("parallel", "arbitrary")),
        name=name,
    )(x, w)


def _rwkv_prep_kernel(*refs, seq_rows, pad, tm, dr, n_heads, d_dec, d_icl, sample):
    if sample:
        (rkv_ref, lora_ref, sp_rkv_ref, sp_lora_ref, mu_rkv_ref, mu_lora_ref, wd_ref, wa_ref, wg_ref,
         w0_ref, a0_ref, kk_s_ref, ka_s_ref, rk_ref,
         r_o, w_o, k_o, v_o, kk_o, b_o, g_o, bon_o) = refs
        carry_rkv = carry_lora = None
    else:
        (rkv_ref, lora_ref, mu_rkv_ref, mu_lora_ref, wd_ref, wa_ref, wg_ref,
         w0_ref, a0_ref, kk_s_ref, ka_s_ref, rk_ref,
         r_o, w_o, k_o, v_o, kk_o, b_o, g_o, bon_o, carry_rkv, carry_lora) = refs
    i = pl.program_id(0)
    row = lax.broadcasted_iota(jnp.int32, (tm, 1), 0)
    p_rkv = rkv_ref[...]
    p_lora = lora_ref[...]
    if sample:
        first = _imod(row, seq_rows) == 0
        prev_rkv = jnp.where(first, sp_rkv_ref[...], pltpu.roll(p_rkv, 1, 0))
        prev_lora = jnp.where(first, sp_lora_ref[...], pltpu.roll(p_lora, 1, 0))
    else:
        @pl.when(i == 0)
        def _():
            carry_rkv[...] = jnp.zeros_like(carry_rkv)
            carry_lora[...] = jnp.zeros_like(carry_lora)

        pos = (i * tm) % seq_rows + row
        real = pos >= pad
        p_rkv = jnp.where(real, p_rkv, 0.0)
        p_lora = jnp.where(real, p_lora, 0.0)
        prev_rkv = jnp.where(row == 0, carry_rkv[0:1, :], pltpu.roll(p_rkv, 1, 0))
        prev_lora = jnp.where(row == 0, carry_lora[0:1, :], pltpu.roll(p_lora, 1, 0))
        prev_rkv = jnp.where(pos > pad, prev_rkv, 0.0)
        prev_lora = jnp.where(pos > pad, prev_lora, 0.0)
        carry_rkv[0:1, :] = p_rkv[tm - 1:tm, :]
        carry_lora[0:1, :] = p_lora[tm - 1:tm, :]

    xs_rkv = p_rkv + (prev_rkv - p_rkv) * mu_rkv_ref[...]
    xl = p_lora + (prev_lora - p_lora) * mu_lora_ref[...]
    r = xs_rkv[:, 0:dr]
    k = xs_rkv[:, dr:2 * dr]
    v = xs_rkv[:, 2 * dr:3 * dr]

    dec = _dot(jnp.tanh(xl).astype(BF16), wd_ref[...])
    a_pre = _dot(xl.astype(BF16), wa_ref[...])
    g = _dot(_sigmoid(xl).astype(BF16), wg_ref[...])
    w_log = -_softplus(-(w0_ref[...] + dec)) - 0.5
    decay = jnp.exp(-jnp.exp(w_log))
    a = _sigmoid(a0_ref[...] + a_pre)

    sel = _head_select(dr, n_heads)
    exp = _head_expand(n_heads, dr)
    kk = k * kk_s_ref[...]
    nrm = jnp.sqrt(_head_sum_bcast(kk * kk, sel, exp))
    kk = kk / jnp.maximum(nrm, 1e-12)
    k_mod = k * (1.0 + (a - 1.0) * ka_s_ref[...])
    bonus = _head_sum_bcast(r * k_mod * rk_ref[...], sel, exp) * v

    r_o[...] = r
    w_o[...] = decay
    k_o[...] = k_mod
    v_o[...] = v
    kk_o[...] = kk
    b_o[...] = kk * a
    g_o[...] = g
    bon_o[...] = bonus


def _rwkv_prep(p, sp_rkv, sp_lora, prm, *, seq_rows, pad, sample):
    m = p.shape[0]
    dr = prm["dr"]
    dl = prm["lora_w"]
    tm = _pick(m if sample else seq_rows, (128, 64, 32, 16, 8))
    assert tm % seq_rows == 0 or not sample
    lora_blk = (3 * dr + 3 * prm["df"]) // dl
    row_rkv = pl.BlockSpec((tm, 3 * dr), lambda i: (i, 0))
    row_lora = pl.BlockSpec((tm, dl), lambda i: (i, lora_blk))
    full = lambda a: pl.BlockSpec(a.shape, lambda i: (0,) * a.ndim)
    consts = [prm["mu_rkv"], prm["mu_lora"], prm["wd"], prm["wa"], prm["wg"], prm["w0"], prm["a0"],
              prm["k_k"], prm["k_a"], prm["r_k"]]
    in_specs = [row_rkv, row_lora]
    args = [p, p]
    if sample:
        in_specs += [pl.BlockSpec((tm, 3 * dr), lambda i: (i, 0)), pl.BlockSpec((tm, dl), lambda i: (i, 0))]
        args += [sp_rkv, sp_lora]
    in_specs += [full(c) for c in consts]
    args += consts
    out_spec = pl.BlockSpec((tm, dr), lambda i: (i, 0))
    scratch = [] if sample else [pltpu.VMEM((8, 3 * dr), F32), pltpu.VMEM((8, dl), F32)]
    return pl.pallas_call(
        functools.partial(_rwkv_prep_kernel, seq_rows=seq_rows, pad=pad, tm=tm, dr=dr,
                          n_heads=dr // HEAD_DIM, d_dec=prm["d_dec"], d_icl=prm["d_icl"], sample=sample),
        grid=(m // tm,),
        in_specs=in_specs,
        out_specs=[out_spec] * 8,
        out_shape=[jax.ShapeDtypeStruct((m, dr), F32)] * 8,
        scratch_shapes=scratch,
        compiler_params=_cparams(("arbitrary",)),
        name="rwkv_prep_sample" if sample else "rwkv_prep_prompt",
    )(*args)


CHAINS = LANES // 2
KH = HEAD_DIM // 2


def _rwkv_scan_kernel(r_ref, w_ref, k_ref, kk_ref, b_ref, v_ref, s0_ref, o_ref, sf_ref, s_scr, *, tb):
    t_blk = pl.program_id(1)

    @pl.when(t_blk == 0)
    def _():
        s_scr[...] = s0_ref[...]

    def step(t, carry):
        r = r_ref[t]
        w = w_ref[t]
        k = k_ref[t]
        kk = kk_ref[t]
        b = b_ref[t]
        for vi in range(HEAD_DIM):
            sv = s_scr[vi]
            skk = jnp.sum(sv * kk, axis=0, keepdims=True)
            skk = skk + pltpu.roll(skk, CHAINS, 1)
            vrow = v_ref[t, pl.ds(vi, 1), :]
            sv = sv * w - skk * b + vrow * k
            s_scr[vi] = sv
            o = jnp.sum(sv * r, axis=0, keepdims=True)
            o_ref[t, pl.ds(vi, 1), :] = o + pltpu.roll(o, CHAINS, 1)
        return carry

    lax.fori_loop(0, tb, step, 0)

    @pl.when(t_blk == pl.num_programs(1) - 1)
    def _():
        sf_ref[...] = s_scr[...]


def _to_chain_layout(x, n_seq, t):
    h = x.shape[1] // HEAD_DIM
    g = (n_seq * h) // CHAINS
    x = x.reshape(n_seq, t, h, 2, KH).transpose(1, 4, 0, 2, 3)
    x = x.reshape(t, KH, g, CHAINS, 2).transpose(0, 1, 2, 4, 3)
    return x.reshape(t, KH, g * LANES)


def _value_to_chain_layout(x, n_seq, t):
    h = x.shape[1] // HEAD_DIM
    g = (n_seq * h) // CHAINS
    x = x.reshape(n_seq, t, h, HEAD_DIM).transpose(1, 3, 0, 2).reshape(t, HEAD_DIM, g, 1, CHAINS)
    return jnp.broadcast_to(x, (t, HEAD_DIM, g, 2, CHAINS)).reshape(t, HEAD_DIM, g * LANES)


def _value_from_chain_layout(o, n_seq, t, h):
    g = (n_seq * h) // CHAINS
    o = o.reshape(t, HEAD_DIM, g, 2, CHAINS)[:, :, :, 0, :].reshape(t, HEAD_DIM, n_seq, h)
    return o.transpose(2, 0, 3, 1).reshape(n_seq * t, h * HEAD_DIM)


def _state_to_chain_layout(s):
    n_seq, h = s.shape[:2]
    g = (n_seq * h) // CHAINS
    s = s.reshape(g, CHAINS, HEAD_DIM, 2, KH).transpose(2, 4, 0, 3, 1)
    return s.reshape(HEAD_DIM, KH, g * LANES)


def _state_from_chain_layout(s, n_seq, h):
    g = (n_seq * h) // CHAINS
    s = s.reshape(HEAD_DIM, KH, g, 2, CHAINS).transpose(2, 4, 0, 3, 1)
    return s.reshape(n_seq, h, HEAD_DIM, HEAD_DIM)


def _rwkv_scan(r, w, k, kk, b, v, s0, n_seq, t):
    h = r.shape[1] // HEAD_DIM
    g = (n_seq * h) // CHAINS
    tb = _pick(t, (64, 32, 16, 8, 4, 2, 1))
    ins = [_to_chain_layout(a, n_seq, t) for a in (r, w, k, kk, b)]
    vin = _value_to_chain_layout(v, n_seq, t)
    sin = _state_to_chain_layout(s0)
    kspec = pl.BlockSpec((tb, KH, LANES), lambda gi, ti: (ti, 0, gi))
    vspec = pl.BlockSpec((tb, HEAD_DIM, LANES), lambda gi, ti: (ti, 0, gi))
    sspec = pl.BlockSpec((HEAD_DIM, KH, LANES), lambda gi, ti: (0, 0, gi))
    o, sf = pl.pallas_call(
        functools.partial(_rwkv_scan_kernel, tb=tb),
        grid=(g, t // tb),
        in_specs=[kspec] * 5 + [vspec, sspec],
        out_specs=[vspec, sspec],
        out_shape=[jax.ShapeDtypeStruct((t, HEAD_DIM, g * LANES), F32),
                   jax.ShapeDtypeStruct((HEAD_DIM, KH, g * LANES), F32)],
        scratch_shapes=[pltpu.VMEM((HEAD_DIM, KH, LANES), F32)],
        compiler_params=_cparams(("parallel", "arbitrary")),
        name="rwkv_scan",
    )(*ins, vin, sin)
    return _value_from_chain_layout(o, n_seq, t, h), _state_from_chain_layout(sf, n_seq, h)


def _rwkv_post_kernel(o_ref, g_ref, bon_ref, gg_ref, gb_ref, out_ref, *, dr):
    n_heads = dr // HEAD_DIM
    sel = _head_select(dr, n_heads)
    exp = _head_expand(n_heads, dr)
    o = o_ref[...]
    mu = _head_sum_bcast(o, sel, exp) * (1.0 / HEAD_DIM)
    oc = o - mu
    var = _head_sum_bcast(oc * oc, sel, exp) * (1.0 / HEAD_DIM)
    y = oc * lax.rsqrt(var + GN_EPS) * gg_ref[...] + gb_ref[...]
    out_ref[...] = ((y + bon_ref[...]) * g_ref[...]).astype(BF16)


def _rwkv_post(o, g, bonus, gn_g, gn_b):
    m, dr = o.shape
    tm = _pick(m, (256, 128, 64, 32, 16, 8))
    row = pl.BlockSpec((tm, dr), lambda i: (i, 0))
    vec = pl.BlockSpec((1, dr), lambda i: (0, 0))
    return pl.pallas_call(
        functools.partial(_rwkv_post_kernel, dr=dr),
        grid=(m // tm,),
        in_specs=[row, row, row, vec, vec],
        out_specs=row,
        out_shape=jax.ShapeDtypeStruct((m, dr), BF16),
        compiler_params=_cparams(("parallel",)),
        name="rwkv_post",
    )(o, g, bonus, gn_g.reshape(1, dr), gn_b.reshape(1, dr))


def _logf_kernel(lora_ref, bf_ref, lf_ref, c_ref, carry, *, tb, pad, f_off, n_heads):
    i = pl.program_id(1)

    @pl.when(i == 0)
    def _():
        carry[...] = jnp.zeros_like(carry)

    z = lora_ref[:, f_off:f_off + n_heads] + bf_ref[...]
    lf = -_softplus(-z)
    pos = i * tb + lax.broadcasted_iota(jnp.int32, (tb, 1), 0)
    lf = jnp.where(pos >= pad, lf, 0.0)
    rr = lax.broadcasted_iota(jnp.int32, (tb, tb), 0)
    cc = lax.broadcasted_iota(jnp.int32, (tb, tb), 1)
    tril = jnp.where(cc <= rr, 1.0, 0.0).astype(BF16)
    c = _dot_01x(tril, lf) + carry[0:1, :]
    lf_ref[...] = lf
    c_ref[...] = c
    carry[0:1, :] = c[tb - 1:tb, :]


def _logf_cumsum(p, b_f, prm, n_seq, seq_rows, pad):
    dl = prm["lora_w"]
    hf = b_f.shape[-1]
    tb = _pick(seq_rows, (128, 64, 32, 16, 8))
    lora_blk = (3 * prm["dr"] + 3 * prm["df"]) // dl
    p3 = p.reshape(n_seq, seq_rows, p.shape[-1])
    out = pl.BlockSpec((None, tb, hf), lambda s, i: (s, i, 0))
    return pl.pallas_call(
        functools.partial(_logf_kernel, tb=tb, pad=pad, f_off=prm["f_off"], n_heads=hf),
        grid=(n_seq, seq_rows // tb),
        in_specs=[pl.BlockSpec((None, tb, dl), lambda s, i: (s, i, lora_blk)),
                  pl.BlockSpec((1, hf), lambda s, i: (0, 0))],
        out_specs=[out, out],
        out_shape=[jax.ShapeDtypeStruct((n_seq, seq_rows, hf), F32)] * 2,
        scratch_shapes=[pltpu.VMEM((8, hf), F32)],
        compiler_params=_cparams(("parallel", "arbitrary")),
        name="logf_cumsum",
    )(p3, b_f.reshape(1, hf))


LOG2E = 1.4426950408889634


def _bf16_terms(x):
    hi = x.astype(BF16)
    r = x - hi.astype(F32)
    mid = r.astype(BF16)
    lo = (r - mid.astype(F32)).astype(BF16)
    return hi.astype(F32), mid.astype(F32), lo.astype(F32)


def _augment(x, own, lane, first, ones_at, terms):
    t0, t1, t2 = terms
    a = first
    extra = jnp.where(lane == a + (3 - ones_at), t0, jnp.where(lane == a + (4 - ones_at), t1,
            jnp.where(lane == a + (5 - ones_at), t2, 0.0)))
    extra = jnp.where((lane >= a + ones_at) & (lane < a + ones_at + 3), 1.0, extra)
    return jnp.where(own, x, extra).astype(BF16)


def _fox_prompt_kernel(q_ref, k_ref, v_ref, cq_ref, ck_ref, o_ref, ka_scr, va_scr, m_scr, acc_scr, *, tq, scale2):
    i = pl.program_id(2)
    seq_rows = k_ref.shape[0]

    @pl.when(i == 0)
    def _():
        lane_k = lax.broadcasted_iota(jnp.int32, (seq_rows, LANES), 1)
        k = k_ref[...]
        v = v_ref[...]
        for h in range(2):
            own = (lane_k < HEAD_DIM) if h == 0 else (lane_k >= HEAD_DIM)
            first = HEAD_DIM * (1 - h)
            ka_scr[h] = _augment(k, own, lane_k, first, 0, _bf16_terms(-ck_ref[:, h:h + 1]))
            va_scr[h] = jnp.where(own, v, 1.0).astype(BF16)

    lane = lax.broadcasted_iota(jnp.int32, (tq, LANES), 1)
    q = q_ref[...] * scale2
    qa = []
    for h in range(2):
        own = (lane < HEAD_DIM) if h == 0 else (lane >= HEAD_DIM)
        qa.append(_augment(q, own, lane, HEAD_DIM * (1 - h), 3, _bf16_terms(cq_ref[:, h:h + 1])))
    m_scr[...] = jnp.full_like(m_scr, NEG_INF)
    acc_scr[...] = jnp.zeros_like(acc_scr)

    def block(j, width, diagonal):
        off = pl.multiple_of(j * tq, tq)
        for h in range(2):
            s = _dot_nt(qa[h], ka_scr[h, pl.ds(off, width), :])
            if diagonal:
                rr = lax.broadcasted_iota(jnp.int32, (tq, width), 0)
                cc = lax.broadcasted_iota(jnp.int32, (tq, width), 1)
                s = jnp.where(cc <= rr, s, NEG_INF)
            cols = [s[:, c * LANES:(c + 1) * LANES] for c in range(width // LANES)]
            m_blk = jnp.max(functools.reduce(jnp.maximum, cols), axis=-1, keepdims=True)
            m_prev = m_scr[h]
            m_new = jnp.maximum(m_prev, m_blk)
            p = jnp.concatenate([jnp.exp2(c - m_new) for c in cols], axis=1).astype(BF16)
            acc_scr[h] = jnp.exp2(m_prev - m_new) * acc_scr[h] + _dot(p, va_scr[h, pl.ds(off, width), :])
            m_scr[h] = m_new

    def body(jj, carry):
        block(2 * jj, 2 * tq, False)
        return carry

    lax.fori_loop(0, i // 2, body, 0)

    @pl.when(i % 2 == 1)
    def _():
        block(i - 1, tq, False)

    block(i, tq, True)
    a0 = acc_scr[0]
    a1 = acc_scr[1]
    o = jnp.where(lane < HEAD_DIM, a0 / pltpu.roll(a0, HEAD_DIM, 1), a1 / pltpu.roll(a1, HEAD_DIM, 1))
    o_ref[...] = o.astype(BF16)


def _fox_prompt(p, c, prm, n_seq, seq_rows, pad):
    dr, df = prm["dr"], prm["df"]
    hf = df // HEAD_DIM
    n_pairs = hf // 2
    tq = _pick(seq_rows, (384, 256, 128))
    nq = seq_rows // tq
    p3 = p.reshape(n_seq, seq_rows, p.shape[-1])
    c4 = (c * LOG2E).reshape(n_seq, seq_rows, n_pairs, 2)
    cq = c4.transpose(0, 2, 1, 3)
    is_pad = (jnp.arange(seq_rows) < pad)[None, :, None, None]
    ck = jnp.where(is_pad, -NEG_INF, c4).transpose(0, 2, 1, 3)
    q0, k0, v0 = (3 * dr) // LANES, (3 * dr + df) // LANES, (3 * dr + 2 * df) // LANES
    return pl.pallas_call(
        functools.partial(_fox_prompt_kernel, tq=tq, scale2=HEAD_DIM ** -0.5 * LOG2E),
        grid=(n_seq, n_pairs, nq),
        in_specs=[pl.BlockSpec((None, tq, LANES), lambda b, pr, i: (b, i, q0 + pr)),
                  pl.BlockSpec((None, seq_rows, LANES), lambda b, pr, i: (b, 0, k0 + pr)),
                  pl.BlockSpec((None, seq_rows, LANES), lambda b, pr, i: (b, 0, v0 + pr)),
                  pl.BlockSpec((None, None, tq, 2), lambda b, pr, i: (b, pr, i, 0)),
                  pl.BlockSpec((None, None, seq_rows, 2), lambda b, pr, i: (b, pr, 0, 0))],
        out_specs=pl.BlockSpec((None, tq, LANES), lambda b, pr, i: (b, i, pr)),
        out_shape=jax.ShapeDtypeStruct((n_seq, seq_rows, df), BF16),
        scratch_shapes=[pltpu.VMEM((2, seq_rows, LANES), BF16), pltpu.VMEM((2, seq_rows, LANES), BF16),
                        pltpu.VMEM((2, tq, LANES), F32), pltpu.VMEM((2, tq, LANES), F32)],
        compiler_params=_cparams(("parallel", "parallel", "arbitrary")),
        name="fox_prompt",
    )(p3, p3, p3, cq, ck).reshape(n_seq * seq_rows, df)


def _fox_sample_kernel(pt_ref, q_ref, kn_ref, vn_ref, cn_ref, cnrow_ref, *rest, pp, page, nq, hf, scale):
    lf_refs = rest[:pp]
    k_refs = rest[pp:2 * pp]
    v_refs = rest[2 * pp:3 * pp]
    o_ref, q_scr, m_scr, l_scr, acc_scr, carry_scr = rest[3 * pp:]
    step = pl.program_id(1)
    ng = hf // SUBLANES
    gr = SUBLANES * nq
    tpl = LANES // SUBLANES
    nc = page // tpl
    kpg = page * SUBLANES

    @pl.when(step == 0)
    def _():
        q = q_ref[...] * scale
        for h in range(hf):
            q_scr[h * nq:(h + 1) * nq, :] = q[:, h * HEAD_DIM:(h + 1) * HEAD_DIM]
        m_scr[...] = jnp.full_like(m_scr, NEG_INF)
        l_scr[...] = jnp.zeros_like(l_scr)
        acc_scr[...] = jnp.zeros_like(acc_scr)
        carry_scr[...] = jnp.zeros_like(carry_scr)

    wr = lax.broadcasted_iota(jnp.int32, (LANES, 2 * LANES), 0)
    wc = lax.broadcasted_iota(jnp.int32, (LANES, 2 * LANES), 1)
    same_head = _imod(wr, SUBLANES) == _imod(wc, SUBLANES)
    later = _idiv(wr, SUBLANES) > _idiv(_imod(wc, LANES), SUBLANES)
    w2 = jnp.where(same_head & ((wc >= LANES) | later), 1.0, 0.0).astype(BF16)
    nr = ng * nc
    gr_i = lax.broadcasted_iota(jnp.int32, (2 * nr, nr), 0)
    gc_i = lax.broadcasted_iota(jnp.int32, (2 * nr, nr), 1)
    rr = _imod(gr_i, nr)
    lg = jnp.where((_idiv(rr, nc) == _idiv(gc_i, nc)) & ((gr_i >= nr) | (_imod(gc_i, nc) > _imod(rr, nc))),
                   1.0, 0.0).astype(BF16)
    carry = carry_scr[...]
    later_sum = [None] * pp
    for x in reversed(range(pp)):
        r = _dot_x01(lf_refs[x][...], w2)
        r2 = _dot_01x(lg, r[:, LANES:])
        later_sum[x] = r[:, :LANES] + r2[:nr] + carry
        carry = carry + r2[nr:]
    carry_scr[...] = carry

    row_head = _idiv(lax.broadcasted_iota(jnp.int32, (gr, pp * kpg), 0), nq)
    lane_head = _imod(lax.broadcasted_iota(jnp.int32, (gr, pp * kpg), 1), SUBLANES)
    own = row_head == lane_head
    for g in range(ng):
        rows = slice(g * gr, (g + 1) * gr)
        heads = slice(g * SUBLANES, (g + 1) * SUBLANES)
        qg = q_scr[rows, :].astype(BF16)
        s = jnp.concatenate(
            [_dot_nt(qg, k_refs[x][:, heads, :].reshape(kpg, HEAD_DIM).astype(BF16)) for x in range(pp)], axis=1)
        bias = jnp.concatenate([later_sum[x][g * nc + c:g * nc + c + 1, :]
                                for x in range(pp) for c in range(nc)], axis=1)
        s = jnp.where(own, s + cn_ref[rows, :] + bias, NEG_INF)
        m_prev = m_scr[rows, :]
        m_new = jnp.maximum(m_prev, jnp.max(s, axis=-1, keepdims=True))
        alpha = jnp.exp(m_prev - m_new)
        p = jnp.exp(s - m_new)
        l_scr[rows, :] = alpha * l_scr[rows, :] + jnp.sum(p, axis=-1, keepdims=True)
        m_scr[rows, :] = m_new
        acc = alpha * acc_scr[rows, :]
        for x in range(pp):
            vg = v_refs[x][:, heads, :].reshape(kpg, HEAD_DIM).astype(BF16)
            acc = acc + _dot(p[:, x * kpg:(x + 1) * kpg].astype(BF16), vg)
        acc_scr[rows, :] = acc

    @pl.when(step == pl.num_programs(1) - 1)
    def _():
        kn = kn_ref[...].astype(BF16)
        vn = vn_ref[...].astype(BF16)
        head = lambda a, h: a[:, h * HEAD_DIM:(h + 1) * HEAD_DIM]
        sn = jnp.stack([_dot_nt(q_scr[h * nq:(h + 1) * nq, :].astype(BF16), head(kn, h)) for h in range(hf)])
        qi = lax.broadcasted_iota(jnp.int32, (hf, nq, nq), 1)
        ki = lax.broadcasted_iota(jnp.int32, (hf, nq, nq), 2)
        sn = jnp.where(ki <= qi, sn + cnrow_ref[...], NEG_INF).reshape(hf * nq, nq)
        m_prev = m_scr[...]
        m_new = jnp.maximum(m_prev, jnp.max(sn, axis=-1, keepdims=True))
        alpha = jnp.exp(m_prev - m_new)
        pn = jnp.exp(sn - m_new)
        l = alpha * l_scr[...] + jnp.sum(pn, axis=-1, keepdims=True)
        pv = jnp.concatenate([_dot(pn[h * nq:(h + 1) * nq, :].astype(BF16), head(vn, h)) for h in range(hf)],
                             axis=0)
        o = (alpha * acc_scr[...] + pv) / l
        o_ref[...] = jnp.concatenate([o[h * nq:(h + 1) * nq, :] for h in range(hf)], axis=1).astype(BF16)


def _fox_sample(p_s, c_new, cache_k, cache_v, cache_lft, page_table_flat, prm, n_seq, nq, n_pages, layer):
    dr, df = prm["dr"], prm["df"]
    hf = df // HEAD_DIM
    page = cache_k.shape[2]
    assert hf % SUBLANES == 0 and page % (LANES // SUBLANES) == 0
    pp = _pick(n_pages, (2, 1))
    npb = n_pages // pp
    rows = hf * nq
    lf_rows = cache_lft.shape[2]
    p3 = p_s.reshape(n_seq, nq, p_s.shape[-1])
    cn_t = c_new.transpose(0, 2, 1)
    cn = cn_t.reshape(n_seq, rows, 1)
    cn_row = cn_t[..., None] - cn_t[:, :, None, :]
    qb, kb, vb = (3 * dr) // df, (3 * dr + df) // df, (3 * dr + 2 * df) // df
    seq_blk = lambda blk: pl.BlockSpec((None, nq, df), lambda b, i, pt: (b, 0, blk))

    def page_of(b, i, pt, x):
        return pt[b * n_pages + (npb - 1 - i) * pp + x]

    def kv_spec(x):
        return pl.BlockSpec((None, None, page, hf, HEAD_DIM),
                            lambda b, i, pt: (layer, page_of(b, i, pt, x), 0, 0, 0))

    def lf_spec(x):
        return pl.BlockSpec((None, None, lf_rows, LANES), lambda b, i, pt: (layer, page_of(b, i, pt, x), 0, 0))

    grid_spec = pltpu.PrefetchScalarGridSpec(
        num_scalar_prefetch=1,
        grid=(n_seq, npb),
        in_specs=[seq_blk(qb), seq_blk(kb), seq_blk(vb),
                  pl.BlockSpec((None, rows, 1), lambda b, i, pt: (b, 0, 0)),
                  pl.BlockSpec((None, hf, nq, nq), lambda b, i, pt: (b, 0, 0, 0))]
                 + [lf_spec(x) for x in range(pp)] + [kv_spec(x) for x in range(pp)] * 2,
        out_specs=pl.BlockSpec((None, nq, df), lambda b, i, pt: (b, 0, 0)),
        scratch_shapes=[pltpu.VMEM((rows, HEAD_DIM), F32), pltpu.VMEM((rows, 1), F32),
                        pltpu.VMEM((rows, 1), F32), pltpu.VMEM((rows, HEAD_DIM), F32),
                        pltpu.VMEM((lf_rows, LANES), F32)],
    )
    out = pl.pallas_call(
        functools.partial(_fox_sample_kernel, pp=pp, page=page, nq=nq, hf=hf, scale=HEAD_DIM ** -0.5),
        grid_spec=grid_spec,
        out_shape=jax.ShapeDtypeStruct((n_seq, nq, df), BF16),
        compiler_params=_cparams(("parallel", "arbitrary")),
        name="fox_sample",
    )(page_table_flat, p3, p3, p3, cn, cn_row, *([cache_lft] * pp), *([cache_k] * pp), *([cache_v] * pp))
    return out.reshape(n_seq * nq, df)


def _gelu_tanh(x):
    return 0.5 * x * (1.0 + jnp.tanh(0.7978845608028654 * (x + 0.044715 * (x * x * x))))


def _ffn_in_kernel(*refs, tm, tn, seq_rows, pad, sample):
    if sample:
        x_ref, wu_ref, wg_ref, cw_ref, cb_ref, p1_ref, p2_ref, h_ref, u_ref = refs
    else:
        x_ref, wu_ref, wg_ref, cw_ref, cb_ref, h_ref, tail_ref, carry = refs
    i = pl.program_id(0)
    j = pl.program_id(1)
    x = x_ref[...]
    u = _dot(x, wu_ref[...])
    gate = _dot(x, wg_ref[...])
    row = lax.broadcasted_iota(jnp.int32, (tm, 1), 0)
    if sample:
        t = _imod(row, seq_rows)
        u1 = jnp.where(t == 0, p1_ref[...], pltpu.roll(u, 1, 0))
        u2 = jnp.where(t < 2, p2_ref[...], pltpu.roll(u, 2, 0))
        u_ref[...] = u
    else:
        @pl.when(i == 0)
        def _():
            carry[:, pl.ds(pl.multiple_of(j * tn, tn), tn)] = jnp.zeros((8, tn), F32)

        pos = (i * tm) % seq_rows + row
        u = jnp.where(pos >= pad, u, 0.0)
        prev = carry[:, pl.ds(pl.multiple_of(j * tn, tn), tn)]
        u1 = jnp.where(row == 0, prev[7:8, :], pltpu.roll(u, 1, 0))
        u2 = jnp.where(row == 0, prev[6:7, :], jnp.where(row == 1, prev[7:8, :], pltpu.roll(u, 2, 0)))
        tail = u[tm - 8:tm, :]
        carry[:, pl.ds(pl.multiple_of(j * tn, tn), tn)] = tail
        tail_ref[...] = tail
    c = cb_ref[...] + cw_ref[0:1, :] * u2 + cw_ref[1:2, :] * u1 + cw_ref[2:3, :] * u
    h_ref[...] = (_gelu_tanh(c) * gate).astype(BF16)


def _ffn_in(x1b, w_ffn_in_b, conv_w, conv_b, p1, p2, *, seq_rows, pad, sample):
    m, d = x1b.shape
    dff = conv_w.shape[-1]
    tn = _pick(dff, (256, 128))
    nj = dff // tn
    tm = m if sample else _pick(seq_rows, (528, 512, 384, 256, 128, 64, 32, 16, 8))
    ni = m // tm
    in_specs = [pl.BlockSpec((tm, d), lambda i, j: (i, 0)),
                pl.BlockSpec((d, tn), lambda i, j: (0, j)),
                pl.BlockSpec((d, tn), lambda i, j: (0, j + nj)),
                pl.BlockSpec((3, tn), lambda i, j: (0, j)),
                pl.BlockSpec((1, tn), lambda i, j: (0, j))]
    args = [x1b, w_ffn_in_b, w_ffn_in_b, conv_w, conv_b.reshape(1, dff)]
    tile = pl.BlockSpec((tm, tn), lambda i, j: (i, j))
    if sample:
        in_specs += [tile, tile]
        args += [p1, p2]
        out_specs = [tile, tile]
        out_shape = [jax.ShapeDtypeStruct((m, dff), BF16), jax.ShapeDtypeStruct((m, dff), F32)]
        scratch = []
    else:
        out_specs = [tile, pl.BlockSpec((8, tn), lambda i, j: (i, j))]
        out_shape = [jax.ShapeDtypeStruct((m, dff), BF16), jax.ShapeDtypeStruct((ni * 8, dff), F32)]
        scratch = [pltpu.VMEM((8, dff), F32)]
    return pl.pallas_call(
        functools.partial(_ffn_in_kernel, tm=tm, tn=tn, seq_rows=seq_rows, pad=pad, sample=sample),
        grid=(ni, nj),
        in_specs=in_specs,
        out_specs=out_specs,
        out_shape=out_shape,
        scratch_shapes=scratch,
        compiler_params=_cparams(("arbitrary", "arbitrary")),
        name="ffn_in_sample" if sample else "ffn_in_prompt",
    )(*args)


def _layer_params(lw, d_model):
    (w_in, b_f, mu_shift, w0, w_decay_up, a0, w_iclr_up, w_gate_up, k_k, k_a, r_k,
     gn_g, gn_b, w_out, ln1_g, ln1_b, w_ffn_in, conv_w, conv_b, w_ffn_out, ln2_g, ln2_b) = lw
    dr = w0.shape[-1]
    hf = b_f.shape[-1]
    df = hf * HEAD_DIM
    d_dec, d_icl, d_gate = w_decay_up.shape[0], w_iclr_up.shape[0], w_gate_up.shape[0]
    rw_cols = 3 * dr + d_dec + d_icl + d_gate
    n_lora = d_dec + d_icl + d_gate
    lora_w = -(-(n_lora + hf) // (4 * LANES)) * (4 * LANES)
    assert (3 * dr + 3 * df) % lora_w == 0 and dr % LANES == 0 and df % LANES == 0
    fox0 = rw_cols
    w_perm = jnp.concatenate(
        [w_in[:, :3 * dr], w_in[:, fox0:fox0 + 3 * df], w_in[:, 3 * dr:rw_cols],
         w_in[:, fox0 + 3 * df:], jnp.zeros((d_model, lora_w - n_lora - hf), w_in.dtype)], axis=1).astype(BF16)

    def pad_rows(w, off):
        return jnp.zeros((lora_w, dr), F32).at[off:off + w.shape[0]].set(w).astype(BF16)

    prm = dict(
        dr=dr, df=df, lora_w=lora_w, d_dec=d_dec, d_icl=d_icl, f_off=n_lora, rw_cols=rw_cols,
        mu_rkv=mu_shift[:3 * dr].reshape(1, -1),
        mu_lora=jnp.zeros((1, lora_w), F32).at[0, :n_lora].set(mu_shift[3 * dr:]),
        wd=pad_rows(w_decay_up, 0), wa=pad_rows(w_iclr_up, d_dec), wg=pad_rows(w_gate_up, d_dec + d_icl),
        w0=w0.reshape(1, dr), a0=a0.reshape(1, dr), k_k=k_k.reshape(1, dr), k_a=k_a.reshape(1, dr),
        r_k=r_k.reshape(1, dr), w_perm=w_perm, w_out=w_out.astype(BF16), w_ffn_in=w_ffn_in.astype(BF16),
        w_ffn_out=w_ffn_out.astype(BF16), b_f=b_f, gn_g=gn_g, gn_b=gn_b, ln1=(ln1_g, ln1_b),
        ln2=(ln2_g, ln2_b), conv_w=conv_w, conv_b=conv_b)
    return prm


def _mm_tiles(m, n, seq_rows):
    tm = _pick(seq_rows, (1056, 1024, 768, 512, 384, 256, 128, 64, 32, 16, 8)) if m > 512 else m
    tn = _pick(n, (640, 512, 256, 128))
    return tm, tn


def _permute_rw_row(row, prm):
    dr = prm["dr"]
    n_lora = prm["f_off"]
    lora = jnp.zeros(row.shape[:-1] + (prm["lora_w"],), F32).at[..., :n_lora].set(row[..., 3 * dr:])
    return row[..., :3 * dr], lora


def _run_group(x2, ln0, prm, alpha, *, n_seq, seq_rows, pad, skip, sample, shift_prev, s0, ffn_prev, attend):
    ln0_g, ln0_b = ln0
    m, d = x2.shape
    dr, df = prm["dr"], prm["df"]
    h_b = _ln0(x2, ln0_g, ln0_b)
    tm, tn = _mm_tiles(m, prm["w_perm"].shape[1], seq_rows)
    p = _matmul(h_b, prm["w_perm"], tm, tn, "in_proj")

    if sample:
        sp_rkv, sp_lora = _permute_rw_row(shift_prev, prm)
        expand = lambda a: jnp.zeros((n_seq, seq_rows, a.shape[-1]), F32).at[:, 0].set(a).reshape(m, -1)
        sp_rkv, sp_lora = expand(sp_rkv), expand(sp_lora)
    else:
        sp_rkv = sp_lora = None
    r, w, k, v, kk, b, g, bonus = _rwkv_prep(p, sp_rkv, sp_lora, prm, seq_rows=seq_rows, pad=pad, sample=sample)
    o_scan, s_last = _rwkv_scan(r, w, k, kk, b, v, s0, n_seq, seq_rows)
    o_rw = _rwkv_post(o_scan, g, bonus, prm["gn_g"], prm["gn_b"])

    lf, c = _logf_cumsum(p, prm["b_f"], prm, n_seq, seq_rows, pad)
    o_fox = attend(p, c)

    mix_in = jnp.concatenate([o_rw, o_fox], axis=1)
    tm, tn = _mm_tiles(m, d, seq_rows)
    mix = _matmul(mix_in, prm["w_out"], tm, tn, "out_proj")
    x1, x1b = _ln1(x2, mix, ln0_g, ln0_b, *prm["ln1"], alpha)

    if sample:
        p1 = jnp.zeros((n_seq, seq_rows, ffn_prev.shape[-1]), F32).at[:, 0].set(ffn_prev[:, 1])
        p2 = p1.at[:, 0].set(ffn_prev[:, 0]).at[:, 1].set(ffn_prev[:, 1])
        p1, p2 = p1.reshape(m, -1), p2.reshape(m, -1)
    else:
        p1 = p2 = None
    hid, u_aux = _ffn_in(x1b, prm["w_ffn_in"], prm["conv_w"], prm["conv_b"], p1, p2,
                         seq_rows=seq_rows, pad=pad, sample=sample)
    dff = hid.shape[1]
    tm_o = _pick(seq_rows, (528, 512, 384, 256, 128, 64, 32, 16, 8)) if m > 512 else m
    ffn = _matmul(hid, prm["w_ffn_out"], tm_o, _pick(d, (256, 128)), "ffn_out")
    y = _ln2(x1, ffn, *prm["ln2"], alpha, n_seq, seq_rows, skip)

    p3 = p.reshape(n_seq, seq_rows, -1)
    hf = df // HEAD_DIM
    k_new = p3[:, pad:, 3 * dr + df:3 * dr + 2 * df].reshape(n_seq, seq_rows - pad, hf, HEAD_DIM)
    v_new = p3[:, pad:, 3 * dr + 2 * df:3 * dr + 3 * df].reshape(n_seq, seq_rows - pad, hf, HEAD_DIM)
    last = p3[:, -1]
    shift_last = jnp.concatenate([last[:, :3 * dr], last[:, 3 * dr + 3 * df:3 * dr + 3 * df + prm["f_off"]]], axis=1)
    if sample:
        conv_state = u_aux.reshape(n_seq, seq_rows, dff)[:, -2:]
    else:
        tiles_per_seq = u_aux.shape[0] // 8 // n_seq
        conv_state = u_aux.reshape(n_seq, tiles_per_seq, 8, dff)[:, -1, -2:]
    return y, (k_new, v_new, lf[:, pad:], s_last, shift_last, conv_state)


def kernel(x_prompt, x_sample, cache_k, cache_v, cache_logf, state_rwkv, state_shift, state_ffn_conv,
           page_table, meta_tokens, ln0_g, ln0_b, w_in, b_f, mu_shift, w0, w_decay_up, a0, w_iclr_up,
           w_gate_up, k_k, k_a, r_k, gn_g, gn_b, w_out, ln1_g, ln1_b, w_ffn_in, conv_w, conv_b,
           w_ffn_out, ln2_g, ln2_b):
    depth = w_in.shape[0]
    assert depth == 1, "the token layout between layers is only wired for a single layer"
    b, seq, d = x_prompt.shape
    db, nq, _ = x_sample.shape
    n_meta = meta_tokens.shape[0]
    alpha = (2 * depth) ** 0.25
    n_pool, page = cache_k.shape[1:3]
    n_pages = page_table.shape[1]
    h_rw = r_k.shape[1]

    t_real = seq + n_meta
    pad = (-t_real) % Q_BLOCK
    t_pad = t_real + pad
    meta = jnp.broadcast_to(meta_tokens[None].astype(x_prompt.dtype), (b, n_meta, d))
    xp = jnp.concatenate([jnp.zeros((b, pad, d), x_prompt.dtype), meta, x_prompt], axis=1).reshape(b * t_pad, d)
    xs = x_sample.reshape(db * nq, d)
    pt_flat = page_table.reshape(-1).astype(jnp.int32)

    outs_p, outs_s = [], []
    for l in range(depth):
        lw = tuple(wt[l] for wt in (w_in, b_f, mu_shift, w0, w_decay_up, a0, w_iclr_up, w_gate_up, k_k, k_a,
                                     r_k, gn_g, gn_b, w_out, ln1_g, ln1_b, w_ffn_in, conv_w, conv_b,
                                     w_ffn_out, ln2_g, ln2_b))
        prm = _layer_params(lw, d)
        hf = prm["df"] // HEAD_DIM

        attend_p = lambda p, c: _fox_prompt(p, c, prm, b, t_pad, pad)
        y_p, st_p = _run_group(xp, (ln0_g, ln0_b), prm, alpha, n_seq=b, seq_rows=t_pad, pad=pad, skip=pad + n_meta, sample=False,
                               shift_prev=None, s0=jnp.zeros((b, h_rw, HEAD_DIM, HEAD_DIM), F32),
                               ffn_prev=None, attend=attend_p)

        tpl = LANES // SUBLANES
        cache_lft = cache_logf.astype(F32).reshape(depth, n_pool, page // tpl, tpl, hf // SUBLANES, SUBLANES)
        cache_lft = cache_lft.transpose(0, 1, 4, 2, 3, 5).reshape(depth, n_pool, -1, LANES)
        attend_s = lambda p, c: _fox_sample(p, c, cache_k, cache_v, cache_lft, pt_flat, prm, db, nq, n_pages, l)
        y_s, st_s = _run_group(xs, (ln0_g, ln0_b), prm, alpha, n_seq=db, seq_rows=nq, pad=0, skip=0, sample=True,
                               shift_prev=state_shift[l], s0=state_rwkv[l], ffn_prev=state_ffn_conv[l],
                               attend=attend_s)
        outs_p.append(st_p)
        outs_s.append(st_s)

    k_p, v_p, lf_p, rw_p, sh_p, cv_p = (jnp.stack([o[i] for o in outs_p]) for i in range(6))
    k_s, v_s, lf_s, rw_s, sh_s, cv_s = (jnp.stack([o[i] for o in outs_s]) for i in range(6))
    y_prompt = y_p
    y_sample = y_s.reshape(db, nq, d)
    return (y_prompt, y_sample, k_p, v_p, lf_p, rw_p, sh_p, cv_p, k_s, v_s, lf_s, rw_s, sh_s, cv_s)
```

```python
import functools

import jax
import jax.numpy as jnp
from jax import lax
from jax.experimental import pallas as pl
from jax.experimental.pallas import tpu as pltpu

F32 = jnp.float32
BF16 = jnp.bfloat16

HEAD_DIM = 64
LANES = 128
SUBLANES = 8
MXU_DEPTH = 256
Q_BLOCK = 128
LN_EPS = 1e-5
GN_EPS = 64e-5
NEG_INF = -1e30
VMEM_LIMIT = 56 * 1024 * 1024


def _cparams(sem):
    return pltpu.CompilerParams(dimension_semantics=sem, vmem_limit_bytes=VMEM_LIMIT)


def _pick(n, candidates):
    for c in candidates:
        if n % c == 0:
            return c
    return n


def _split_bf16(x):
    hi = x.astype(BF16)
    lo = (x - hi.astype(F32)).astype(BF16)
    return hi, lo


def _dot(a, b):
    return jnp.dot(a, b, preferred_element_type=F32)


def _dot_nt(a, b):
    return lax.dot_general(a, b, (((1,), (1,)), ((), ())), preferred_element_type=F32)


def _dot_x01(x, m01):
    hi, lo = _split_bf16(x)
    return _dot(hi, m01) + _dot(lo, m01)


def _dot_01x(m01, x):
    hi, lo = _split_bf16(x)
    return _dot(m01, hi) + _dot(m01, lo)


def _idiv(x, c):
    return lax.shift_right_logical(x, c.bit_length() - 1) if c & (c - 1) == 0 else x // c


def _imod(x, c):
    return (x & (c - 1)) if c & (c - 1) == 0 else x % c


def _head_select(n_cols, n_heads):
    c = lax.broadcasted_iota(jnp.int32, (n_cols, n_heads), 0)
    h = lax.broadcasted_iota(jnp.int32, (n_cols, n_heads), 1)
    return jnp.where(_idiv(c, HEAD_DIM) == h, 1.0, 0.0).astype(BF16)


def _head_expand(n_heads, n_cols):
    h = lax.broadcasted_iota(jnp.int32, (n_heads, n_cols), 0)
    c = lax.broadcasted_iota(jnp.int32, (n_heads, n_cols), 1)
    return jnp.where(_idiv(c, HEAD_DIM) == h, 1.0, 0.0).astype(BF16)


def _head_sum_bcast(x, sel, exp):
    return _dot_x01(_dot_x01(x, sel), exp)


def _softplus(z):
    return jnp.maximum(z, 0.0) + jnp.log1p(jnp.exp(-jnp.abs(z)))


def _sigmoid(z):
    return 1.0 / (1.0 + jnp.exp(-z))


def _ln_rows(x, g, b):
    mu = jnp.mean(x, axis=-1, keepdims=True)
    xc = x - mu
    var = jnp.mean(xc * xc, axis=-1, keepdims=True)
    return xc * lax.rsqrt(var + LN_EPS) * g + b


def _ln0_kernel(x_ref, g_ref, b_ref, o_ref):
    o_ref[...] = _ln_rows(x_ref[...], g_ref[...], b_ref[...]).astype(o_ref.dtype)


def _ln0(x2, g, b):
    m, d = x2.shape
    tm = _pick(m, (528, 512, 384, 256, 128, 64, 32, 16, 8))
    return pl.pallas_call(
        _ln0_kernel,
        grid=(m // tm,),
        in_specs=[pl.BlockSpec((tm, d), lambda i: (i, 0)),
                  pl.BlockSpec((1, d), lambda i: (0, 0)),
                  pl.BlockSpec((1, d), lambda i: (0, 0))],
        out_specs=pl.BlockSpec((tm, d), lambda i: (i, 0)),
        out_shape=jax.ShapeDtypeStruct((m, d), BF16),
        compiler_params=_cparams(("parallel",)),
        name="ln0",
    )(x2, g.reshape(1, d), b.reshape(1, d))


def _ln1_kernel(x_ref, mix_ref, g0_ref, b0_ref, g_ref, b_ref, of_ref, ob_ref, *, alpha):
    h = _ln_rows(x_ref[...], g0_ref[...], b0_ref[...])
    x1 = _ln_rows(alpha * h + mix_ref[...], g_ref[...], b_ref[...])
    of_ref[...] = x1
    ob_ref[...] = x1.astype(BF16)


def _ln1(x2, mix, g0, b0, g, b, alpha):
    m, d = x2.shape
    tm = _pick(m, (264, 256, 128, 64, 32, 16, 8))
    row = pl.BlockSpec((tm, d), lambda i: (i, 0))
    vec = pl.BlockSpec((1, d), lambda i: (0, 0))
    return pl.pallas_call(
        functools.partial(_ln1_kernel, alpha=alpha),
        grid=(m // tm,),
        in_specs=[row, row, vec, vec, vec, vec],
        out_specs=[row, row],
        out_shape=[jax.ShapeDtypeStruct((m, d), F32), jax.ShapeDtypeStruct((m, d), BF16)],
        compiler_params=_cparams(("parallel",)),
        name="ln1",
    )(x2, mix, g0.reshape(1, d), b0.reshape(1, d), g.reshape(1, d), b.reshape(1, d))


def _ln2_kernel(x1_ref, ffn_ref, g_ref, b_ref, o_ref, *, alpha):
    o_ref[...] = _ln_rows(alpha * x1_ref[...] + ffn_ref[...], g_ref[...], b_ref[...])


def _ln2(x1, ffn, g, b, alpha, n_seq, seq_rows, skip_rows):
    d = x1.shape[-1]
    out_rows = seq_rows - skip_rows
    tb = next(c for c in (256, 128, 64, 32, 16, 8) if skip_rows % c == 0 and out_rows % c == 0)
    skip = skip_rows // tb
    x3 = x1.reshape(n_seq, seq_rows, d)
    f3 = ffn.reshape(n_seq, seq_rows, d)
    row_in = pl.BlockSpec((None, tb, d), lambda s, i: (s, i + skip, 0))
    vec = pl.BlockSpec((1, d), lambda s, i: (0, 0))
    return pl.pallas_call(
        functools.partial(_ln2_kernel, alpha=alpha),
        grid=(n_seq, out_rows // tb),
        in_specs=[row_in, row_in, vec, vec],
        out_specs=pl.BlockSpec((None, tb, d), lambda s, i: (s, i, 0)),
        out_shape=jax.ShapeDtypeStruct((n_seq, out_rows, d), F32),
        compiler_params=_cparams(("parallel", "parallel")),
        name="ln2",
    )(x3, f3, g.reshape(1, d), b.reshape(1, d))


def _mm_kernel(x_ref, w_ref, o_ref):
    o_ref[...] = _dot(x_ref[...], w_ref[...])


def _matmul(x, w, tm, tn, name):
    m, k = x.shape
    n = w.shape[1]
    return pl.pallas_call(
        _mm_kernel,
        grid=(m // tm, n // tn),
        in_specs=[pl.BlockSpec((tm, k), lambda i, j: (i, 0)),
                  pl.BlockSpec((k, tn), lambda i, j: (0, j))],
        out_specs=pl.BlockSpec((tm, tn), lambda i, j: (i, j)),
        out_shape=jax.ShapeDtypeStruct((m, n), F32),
        compiler_params=_cparams(("parallel", "arbitrary")),
        name=name,
    )(x, w)


def _rwkv_prep_kernel(*refs, seq_rows, pad, tm, dr, n_heads, d_dec, d_icl, sample):
    if sample:
        (rkv_ref, lora_ref, sp_rkv_ref, sp_lora_ref, mu_rkv_ref, mu_lora_ref, wd_ref, wa_ref, wg_ref,
         w0_ref, a0_ref, kk_s_ref, ka_s_ref, rk_ref,
         r_o, w_o, k_o, v_o, kk_o, b_o, g_o, bon_o) = refs
        carry_rkv = carry_lora = None
    else:
        (rkv_ref, lora_ref, mu_rkv_ref, mu_lora_ref, wd_ref, wa_ref, wg_ref,
         w0_ref, a0_ref, kk_s_ref, ka_s_ref, rk_ref,
         r_o, w_o, k_o, v_o, kk_o, b_o, g_o, bon_o, carry_rkv, carry_lora) = refs
    i = pl.program_id(0)
    row = lax.broadcasted_iota(jnp.int32, (tm, 1), 0)
    p_rkv = rkv_ref[...]
    p_lora = lora_ref[...]
    if sample:
        first = _imod(row, seq_rows) == 0
        prev_rkv = jnp.where(first, sp_rkv_ref[...], pltpu.roll(p_rkv, 1, 0))
        prev_lora = jnp.where(first, sp_lora_ref[...], pltpu.roll(p_lora, 1, 0))
    else:
        @pl.when(i == 0)
        def _():
            carry_rkv[...] = jnp.zeros_like(carry_rkv)
            carry_lora[...] = jnp.zeros_like(carry_lora)

        pos = (i * tm) % seq_rows + row
        real = pos >= pad
        p_rkv = jnp.where(real, p_rkv, 0.0)
        p_lora = jnp.where(real, p_lora, 0.0)
        prev_rkv = jnp.where(row == 0, carry_rkv[0:1, :], pltpu.roll(p_rkv, 1, 0))
        prev_lora = jnp.where(row == 0, carry_lora[0:1, :], pltpu.roll(p_lora, 1, 0))
        prev_rkv = jnp.where(pos > pad, prev_rkv, 0.0)
        prev_lora = jnp.where(pos > pad, prev_lora, 0.0)
        carry_rkv[0:1, :] = p_rkv[tm - 1:tm, :]
        carry_lora[0:1, :] = p_lora[tm - 1:tm, :]

    xs_rkv = p_rkv + (prev_rkv - p_rkv) * mu_rkv_ref[...]
    xl = p_lora + (prev_lora - p_lora) * mu_lora_ref[...]
    r = xs_rkv[:, 0:dr]
    k = xs_rkv[:, dr:2 * dr]
    v = xs_rkv[:, 2 * dr:3 * dr]

    dec = _dot(jnp.tanh(xl).astype(BF16), wd_ref[...])
    a_pre = _dot(xl.astype(BF16), wa_ref[...])
    g = _dot(_sigmoid(xl).astype(BF16), wg_ref[...])
    w_log = -_softplus(-(w0_ref[...] + dec)) - 0.5
    decay = jnp.exp(-jnp.exp(w_log))
    a = _sigmoid(a0_ref[...] + a_pre)

    sel = _head_select(dr, n_heads)
    exp = _head_expand(n_heads, dr)
    kk = k * kk_s_ref[...]
    nrm = jnp.sqrt(_head_sum_bcast(kk * kk, sel, exp))
    kk = kk / jnp.maximum(nrm, 1e-12)
    k_mod = k * (1.0 + (a - 1.0) * ka_s_ref[...])
    bonus = _head_sum_bcast(r * k_mod * rk_ref[...], sel, exp) * v

    r_o[...] = r
    w_o[...] = decay
    k_o[...] = k_mod
    v_o[...] = v
    kk_o[...] = kk
    b_o[...] = kk * a
    g_o[...] = g
    bon_o[...] = bonus


def _rwkv_prep(p, sp_rkv, sp_lora, prm, *, seq_rows, pad, sample):
    m = p.shape[0]
    dr = prm["dr"]
    dl = prm["lora_w"]
    tm = _pick(m if sample else seq_rows, (128, 64, 32, 16, 8))
    assert tm % seq_rows == 0 or not sample
    lora_blk = (3 * dr + 3 * prm["df"]) // dl
    row_rkv = pl.BlockSpec((tm, 3 * dr), lambda i: (i, 0))
    row_lora = pl.BlockSpec((tm, dl), lambda i: (i, lora_blk))
    full = lambda a: pl.BlockSpec(a.shape, lambda i: (0,) * a.ndim)
    consts = [prm["mu_rkv"], prm["mu_lora"], prm["wd"], prm["wa"], prm["wg"], prm["w0"], prm["a0"],
              prm["k_k"], prm["k_a"], prm["r_k"]]
    in_specs = [row_rkv, row_lora]
    args = [p, p]
    if sample:
        in_specs += [pl.BlockSpec((tm, 3 * dr), lambda i: (i, 0)), pl.BlockSpec((tm, dl), lambda i: (i, 0))]
        args += [sp_rkv, sp_lora]
    in_specs += [full(c) for c in consts]
    args += consts
    out_spec = pl.BlockSpec((tm, dr), lambda i: (i, 0))
    scratch = [] if sample else [pltpu.VMEM((8, 3 * dr), F32), pltpu.VMEM((8, dl), F32)]
    return pl.pallas_call(
        functools.partial(_rwkv_prep_kernel, seq_rows=seq_rows, pad=pad, tm=tm, dr=dr,
                          n_heads=dr // HEAD_DIM, d_dec=prm["d_dec"], d_icl=prm["d_icl"], sample=sample),
        grid=(m // tm,),
        in_specs=in_specs,
        out_specs=[out_spec] * 8,
        out_shape=[jax.ShapeDtypeStruct((m, dr), F32)] * 8,
        scratch_shapes=scratch,
        compiler_params=_cparams(("arbitrary",)),
        name="rwkv_prep_sample" if sample else "rwkv_prep_prompt",
    )(*args)


CHAINS = LANES // 2
KH = HEAD_DIM // 2


def _rwkv_scan_kernel(r_ref, w_ref, k_ref, kk_ref, b_ref, v_ref, s0_ref, o_ref, sf_ref, s_scr, *, tb):
    t_blk = pl.program_id(1)

    @pl.when(t_blk == 0)
    def _():
        s_scr[...] = s0_ref[...]

    def step(t, carry):
        r = r_ref[t]
        w = w_ref[t]
        k = k_ref[t]
        kk = kk_ref[t]
        b = b_ref[t]
        for vi in range(HEAD_DIM):
            sv = s_scr[vi]
            skk = jnp.sum(sv * kk, axis=0, keepdims=True)
            skk = skk + pltpu.roll(skk, CHAINS, 1)
            vrow = v_ref[t, pl.ds(vi, 1), :]
            sv = sv * w - skk * b + vrow * k
            s_scr[vi] = sv
            o = jnp.sum(sv * r, axis=0, keepdims=True)
            o_ref[t, pl.ds(vi, 1), :] = o + pltpu.roll(o, CHAINS, 1)
        return carry

    lax.fori_loop(0, tb, step, 0)

    @pl.when(t_blk == pl.num_programs(1) - 1)
    def _():
        sf_ref[...] = s_scr[...]


def _to_chain_layout(x, n_seq, t):
    h = x.shape[1] // HEAD_DIM
    g = (n_seq * h) // CHAINS
    x = x.reshape(n_seq, t, h, 2, KH).transpose(1, 4, 0, 2, 3)
    x = x.reshape(t, KH, g, CHAINS, 2).transpose(0, 1, 2, 4, 3)
    return x.reshape(t, KH, g * LANES)


def _value_to_chain_layout(x, n_seq, t):
    h = x.shape[1] // HEAD_DIM
    g = (n_seq * h) // CHAINS
    x = x.reshape(n_seq, t, h, HEAD_DIM).transpose(1, 3, 0, 2).reshape(t, HEAD_DIM, g, 1, CHAINS)
    return jnp.broadcast_to(x, (t, HEAD_DIM, g, 2, CHAINS)).reshape(t, HEAD_DIM, g * LANES)


def _value_from_chain_layout(o, n_seq, t, h):
    g = (n_seq * h) // CHAINS
    o = o.reshape(t, HEAD_DIM, g, 2, CHAINS)[:, :, :, 0, :].reshape(t, HEAD_DIM, n_seq, h)
    return o.transpose(2, 0, 3, 1).reshape(n_seq * t, h * HEAD_DIM)


def _state_to_chain_layout(s):
    n_seq, h = s.shape[:2]
    g = (n_seq * h) // CHAINS
    s = s.reshape(g, CHAINS, HEAD_DIM, 2, KH).transpose(2, 4, 0, 3, 1)
    return s.reshape(HEAD_DIM, KH, g * LANES)


def _state_from_chain_layout(s, n_seq, h):
    g = (n_seq * h) // CHAINS
    s = s.reshape(HEAD_DIM, KH, g, 2, CHAINS).transpose(2, 4, 0, 3, 1)
    return s.reshape(n_seq, h, HEAD_DIM, HEAD_DIM)


def _rwkv_scan(r, w, k, kk, b, v, s0, n_seq, t):
    h = r.shape[1] // HEAD_DIM
    g = (n_seq * h) // CHAINS
    tb = _pick(t, (64, 32, 16, 8, 4, 2, 1))
    ins = [_to_chain_layout(a, n_seq, t) for a in (r, w, k, kk, b)]
    vin = _value_to_chain_layout(v, n_seq, t)
    sin = _state_to_chain_layout(s0)
    kspec = pl.BlockSpec((tb, KH, LANES), lambda gi, ti: (ti, 0, gi))
    vspec = pl.BlockSpec((tb, HEAD_DIM, LANES), lambda gi, ti: (ti, 0, gi))
    sspec = pl.BlockSpec((HEAD_DIM, KH, LANES), lambda gi, ti: (0, 0, gi))
    o, sf = pl.pallas_call(
        functools.partial(_rwkv_scan_kernel, tb=tb),
        grid=(g, t // tb),
        in_specs=[kspec] * 5 + [vspec, sspec],
        out_specs=[vspec, sspec],
        out_shape=[jax.ShapeDtypeStruct((t, HEAD_DIM, g * LANES), F32),
                   jax.ShapeDtypeStruct((HEAD_DIM, KH, g * LANES), F32)],
        scratch_shapes=[pltpu.VMEM((HEAD_DIM, KH, LANES), F32)],
        compiler_params=_cparams(("parallel", "arbitrary")),
        name="rwkv_scan",
    )(*ins, vin, sin)
    return _value_from_chain_layout(o, n_seq, t, h), _state_from_chain_layout(sf, n_seq, h)


def _rwkv_post_kernel(o_ref, g_ref, bon_ref, gg_ref, gb_ref, out_ref, *, dr):
    n_heads = dr // HEAD_DIM
    sel = _head_select(dr, n_heads)
    exp = _head_expand(n_heads, dr)
    o = o_ref[...]
    mu = _head_sum_bcast(o, sel, exp) * (1.0 / HEAD_DIM)
    oc = o - mu
    var = _head_sum_bcast(oc * oc, sel, exp) * (1.0 / HEAD_DIM)
    y = oc * lax.rsqrt(var + GN_EPS) * gg_ref[...] + gb_ref[...]
    out_ref[...] = ((y + bon_ref[...]) * g_ref[...]).astype(BF16)


def _rwkv_post(o, g, bonus, gn_g, gn_b):
    m, dr = o.shape
    tm = _pick(m, (256, 128, 64, 32, 16, 8))
    row = pl.BlockSpec((tm, dr), lambda i: (i, 0))
    vec = pl.BlockSpec((1, dr), lambda i: (0, 0))
    return pl.pallas_call(
        functools.partial(_rwkv_post_kernel, dr=dr),
        grid=(m // tm,),
        in_specs=[row, row, row, vec, vec],
        out_specs=row,
        out_shape=jax.ShapeDtypeStruct((m, dr), BF16),
        compiler_params=_cparams(("parallel",)),
        name="rwkv_post",
    )(o, g, bonus, gn_g.reshape(1, dr), gn_b.reshape(1, dr))


def _logf_kernel(lora_ref, bf_ref, lf_ref, c_ref, carry, *, tb, pad, f_off, n_heads):
    i = pl.program_id(1)

    @pl.when(i == 0)
    def _():
        carry[...] = jnp.zeros_like(carry)

    z = lora_ref[:, f_off:f_off + n_heads] + bf_ref[...]
    lf = -_softplus(-z)
    pos = i * tb + lax.broadcasted_iota(jnp.int32, (tb, 1), 0)
    lf = jnp.where(pos >= pad, lf, 0.0)
    rr = lax.broadcasted_iota(jnp.int32, (tb, tb), 0)
    cc = lax.broadcasted_iota(jnp.int32, (tb, tb), 1)
    tril = jnp.where(cc <= rr, 1.0, 0.0).astype(BF16)
    c = _dot_01x(tril, lf) + carry[0:1, :]
    lf_ref[...] = lf
    c_ref[...] = c
    carry[0:1, :] = c[tb - 1:tb, :]


def _logf_cumsum(p, b_f, prm, n_seq, seq_rows, pad):
    dl = prm["lora_w"]
    hf = b_f.shape[-1]
    tb = _pick(seq_rows, (128, 64, 32, 16, 8))
    lora_blk = (3 * prm["dr"] + 3 * prm["df"]) // dl
    p3 = p.reshape(n_seq, seq_rows, p.shape[-1])
    out = pl.BlockSpec((None, tb, hf), lambda s, i: (s, i, 0))
    return pl.pallas_call(
        functools.partial(_logf_kernel, tb=tb, pad=pad, f_off=prm["f_off"], n_heads=hf),
        grid=(n_seq, seq_rows // tb),
        in_specs=[pl.BlockSpec((None, tb, dl), lambda s, i: (s, i, lora_blk)),
                  pl.BlockSpec((1, hf), lambda s, i: (0, 0))],
        out_specs=[out, out],
        out_shape=[jax.ShapeDtypeStruct((n_seq, seq_rows, hf), F32)] * 2,
        scratch_shapes=[pltpu.VMEM((8, hf), F32)],
        compiler_params=_cparams(("parallel", "arbitrary")),
        name="logf_cumsum",
    )(p3, b_f.reshape(1, hf))


LOG2E = 1.4426950408889634


def _bf16_terms(x):
    hi = x.astype(BF16)
    r = x - hi.astype(F32)
    mid = r.astype(BF16)
    lo = (r - mid.astype(F32)).astype(BF16)
    return hi.astype(F32), mid.astype(F32), lo.astype(F32)


def _augment(x, own, lane, first, ones_at, terms):
    t0, t1, t2 = terms
    a = first
    extra = jnp.where(lane == a + (3 - ones_at), t0, jnp.where(lane == a + (4 - ones_at), t1,
            jnp.where(lane == a + (5 - ones_at), t2, 0.0)))
    extra = jnp.where((lane >= a + ones_at) & (lane < a + ones_at + 3), 1.0, extra)
    return jnp.where(own, x, extra).astype(BF16)


def _fox_prompt_kernel(q_ref, k_ref, v_ref, cq_ref, ck_ref, o_ref, ka_scr, va_scr, m_scr, acc_scr, *, tq, scale2):
    i = pl.program_id(2)
    seq_rows = k_ref.shape[0]

    @pl.when(i == 0)
    def _():
        lane_k = lax.broadcasted_iota(jnp.int32, (seq_rows, LANES), 1)
        k = k_ref[...]
        v = v_ref[...]
        for h in range(2):
            own = (lane_k < HEAD_DIM) if h == 0 else (lane_k >= HEAD_DIM)
            first = HEAD_DIM * (1 - h)
            ka_scr[h] = _augment(k, own, lane_k, first, 0, _bf16_terms(-ck_ref[:, h:h + 1]))
            va_scr[h] = jnp.where(own, v, 1.0).astype(BF16)

    lane = lax.broadcasted_iota(jnp.int32, (tq, LANES), 1)
    q = q_ref[...] * scale2
    qa = []
    for h in range(2):
        own = (lane < HEAD_DIM) if h == 0 else (lane >= HEAD_DIM)
        qa.append(_augment(q, own, lane, HEAD_DIM * (1 - h), 3, _bf16_terms(cq_ref[:, h:h + 1])))
    m_scr[...] = jnp.full_like(m_scr, NEG_INF)
    acc_scr[...] = jnp.zeros_like(acc_scr)

    def block(j, width, diagonal):
        off = pl.multiple_of(j * tq, tq)
        for h in range(2):
            s = _dot_nt(qa[h], ka_scr[h, pl.ds(off, width), :])
            if diagonal:
                rr = lax.broadcasted_iota(jnp.int32, (tq, width), 0)
                cc = lax.broadcasted_iota(jnp.int32, (tq, width), 1)
                s = jnp.where(cc <= rr, s, NEG_INF)
            cols = [s[:, c * LANES:(c + 1) * LANES] for c in range(width // LANES)]
            m_blk = jnp.max(functools.reduce(jnp.maximum, cols), axis=-1, keepdims=True)
            m_prev = m_scr[h]
            m_new = jnp.maximum(m_prev, m_blk)
            p = jnp.concatenate([jnp.exp2(c - m_new) for c in cols], axis=1).astype(BF16)
            acc_scr[h] = jnp.exp2(m_prev - m_new) * acc_scr[h] + _dot(p, va_scr[h, pl.ds(off, width), :])
            m_scr[h] = m_new

    def body(jj, carry):
        block(2 * jj, 2 * tq, False)
        return carry

    lax.fori_loop(0, i // 2, body, 0)

    @pl.when(i % 2 == 1)
    def _():
        block(i - 1, tq, False)

    block(i, tq, True)
    a0 = acc_scr[0]
    a1 = acc_scr[1]
    o = jnp.where(lane < HEAD_DIM, a0 / pltpu.roll(a0, HEAD_DIM, 1), a1 / pltpu.roll(a1, HEAD_DIM, 1))
    o_ref[...] = o.astype(BF16)


def _fox_prompt(p, c, prm, n_seq, seq_rows, pad):
    dr, df = prm["dr"], prm["df"]
    hf = df // HEAD_DIM
    n_pairs = hf // 2
    tq = _pick(seq_rows, (384, 256, 128))
    nq = seq_rows // tq
    p3 = p.reshape(n_seq, seq_rows, p.shape[-1])
    c4 = (c * LOG2E).reshape(n_seq, seq_rows, n_pairs, 2)
    cq = c4.transpose(0, 2, 1, 3)
    is_pad = (jnp.arange(seq_rows) < pad)[None, :, None, None]
    ck = jnp.where(is_pad, -NEG_INF, c4).transpose(0, 2, 1, 3)
    q0, k0, v0 = (3 * dr) // LANES, (3 * dr + df) // LANES, (3 * dr + 2 * df) // LANES
    return pl.pallas_call(
        functools.partial(_fox_prompt_kernel, tq=tq, scale2=HEAD_DIM ** -0.5 * LOG2E),
        grid=(n_seq, n_pairs, nq),
        in_specs=[pl.BlockSpec((None, tq, LANES), lambda b, pr, i: (b, i, q0 + pr)),
                  pl.BlockSpec((None, seq_rows, LANES), lambda b, pr, i: (b, 0, k0 + pr)),
                  pl.BlockSpec((None, seq_rows, LANES), lambda b, pr, i: (b, 0, v0 + pr)),
                  pl.BlockSpec((None, None, tq, 2), lambda b, pr, i: (b, pr, i, 0)),
                  pl.BlockSpec((None, None, seq_rows, 2), lambda b, pr, i: (b, pr, 0, 0))],
        out_specs=pl.BlockSpec((None, tq, LANES), lambda b, pr, i: (b, i, pr)),
        out_shape=jax.ShapeDtypeStruct((n_seq, seq_rows, df), BF16),
        scratch_shapes=[pltpu.VMEM((2, seq_rows, LANES), BF16), pltpu.VMEM((2, seq_rows, LANES), BF16),
                        pltpu.VMEM((2, tq, LANES), F32), pltpu.VMEM((2, tq, LANES), F32)],
        compiler_params=_cparams(("parallel", "parallel", "arbitrary")),
        name="fox_prompt",
    )(p3, p3, p3, cq, ck).reshape(n_seq * seq_rows, df)


def _fox_sample_kernel(pt_ref, q_ref, kn_ref, vn_ref, cn_ref, cnrow_ref, *rest, pp, page, nq, hf, scale):
    lf_refs = rest[:pp]
    k_refs = rest[pp:2 * pp]
    v_refs = rest[2 * pp:3 * pp]
    o_ref, qbd_scr, m_scr, l_scr, acc_scr, carry_scr = rest[3 * pp:]
    step = pl.program_id(1)
    gh = MXU_DEPTH // HEAD_DIM
    ng = hf // gh
    gr = gh * nq
    gd = gh * HEAD_DIM
    rows = hf * nq
    own = (_idiv(lax.broadcasted_iota(jnp.int32, (gr, gd), 0), nq)
           == _idiv(lax.broadcasted_iota(jnp.int32, (gr, gd), 1), HEAD_DIM))

    @pl.when(step == 0)
    def _():
        q = q_ref[...] * scale
        for g in range(ng):
            qg = jnp.broadcast_to(q[None, :, g * gd:(g + 1) * gd], (gh, nq, gd)).reshape(gr, gd)
            qbd_scr[g] = jnp.where(own, qg, 0.0).astype(BF16)
        m_scr[...] = jnp.full_like(m_scr, NEG_INF)
        l_scr[...] = jnp.zeros_like(l_scr)
        acc_scr[...] = jnp.zeros_like(acc_scr)
        carry_scr[...] = jnp.zeros_like(carry_scr)

    tr = lax.broadcasted_iota(jnp.int32, (page, 2 * page), 0)
    tc = lax.broadcasted_iota(jnp.int32, (page, 2 * page), 1)
    suffix = jnp.where((tc >= page) | (tr > tc), 1.0, 0.0).astype(BF16)
    carry = carry_scr[...]
    later_sum = [None] * pp
    for x in reversed(range(pp)):
        r = _dot_x01(lf_refs[x][...], suffix)
        later_sum[x] = r[:, :page] + carry
        carry = carry + r[:, page:]
    carry_scr[...] = carry
    bias = jnp.concatenate(later_sum, axis=1)
    bias = jnp.broadcast_to(bias[:, None, :], (hf, nq, pp * page)).reshape(rows, pp * page)

    group = lambda ref, g: ref[g * gd:(g + 1) * gd, :].astype(BF16)
    s = jnp.concatenate(
        [jnp.concatenate([_dot(qbd_scr[g], group(k_refs[x], g)) for g in range(ng)], axis=0) for x in range(pp)],
        axis=1)
    s = s + cn_ref[...] + bias
    m_prev = m_scr[...]
    m_new = jnp.maximum(m_prev, jnp.max(s, axis=-1, keepdims=True))
    alpha = jnp.exp(m_prev - m_new)
    p = jnp.exp(s - m_new)
    l_scr[...] = alpha * l_scr[...] + jnp.sum(p, axis=-1, keepdims=True)
    m_scr[...] = m_new
    for g in range(ng):
        rws = slice(g * gr, (g + 1) * gr)
        acc = alpha[rws] * acc_scr[g]
        for x in range(pp):
            acc = acc + _dot_nt(p[rws, x * page:(x + 1) * page].astype(BF16), group(v_refs[x], g))
        acc_scr[g] = acc

    @pl.when(step == pl.num_programs(1) - 1)
    def _():
        q = (q_ref[...] * scale).astype(BF16)
        kn = kn_ref[...].astype(BF16)
        vn = vn_ref[...].astype(BF16)
        head = lambda a, h: a[:, h * HEAD_DIM:(h + 1) * HEAD_DIM]
        sn = jnp.stack([_dot_nt(head(q, h), head(kn, h)) for h in range(hf)])
        qi = lax.broadcasted_iota(jnp.int32, (hf, nq, nq), 1)
        ki = lax.broadcasted_iota(jnp.int32, (hf, nq, nq), 2)
        sn = jnp.where(ki <= qi, sn + cnrow_ref[...], NEG_INF).reshape(rows, nq)
        m_last = m_scr[...]
        m_fin = jnp.maximum(m_last, jnp.max(sn, axis=-1, keepdims=True))
        a_fin = jnp.exp(m_last - m_fin)
        pn = jnp.exp(sn - m_fin)
        inv = 1.0 / (a_fin * l_scr[...] + jnp.sum(pn, axis=-1, keepdims=True))
        pn = (pn * inv).astype(BF16)
        o_new = jnp.concatenate([_dot(pn[h * nq:(h + 1) * nq, :], head(vn, h)) for h in range(hf)], axis=1)
        w_past = a_fin * inv
        o_past = jnp.concatenate(
            [jnp.sum(jnp.where(own, acc_scr[g] * w_past[g * gr:(g + 1) * gr], 0.0).reshape(gh, nq, gd), axis=0)
             for g in range(ng)], axis=1)
        o_ref[...] = (o_past + o_new).astype(BF16)


def _fox_sample(p_s, c_new, cache_k, cache_v, cache_lft, page_table_flat, prm, n_seq, nq, n_pages, layer):
    dr, df = prm["dr"], prm["df"]
    hf = df // HEAD_DIM
    page = cache_k.shape[3]
    gh = MXU_DEPTH // HEAD_DIM
    assert hf % gh == 0
    pp = _pick(n_pages, (4, 2, 1))
    npb = n_pages // pp
    rows = hf * nq
    p3 = p_s.reshape(n_seq, nq, p_s.shape[-1])
    cn_t = c_new.transpose(0, 2, 1)
    cn = cn_t.reshape(n_seq, rows, 1)
    cn_row = cn_t[..., None] - cn_t[:, :, None, :]
    qb, kb, vb = (3 * dr) // df, (3 * dr + df) // df, (3 * dr + 2 * df) // df
    seq_blk = lambda blk: pl.BlockSpec((None, nq, df), lambda b, i, pt: (b, 0, blk))

    def page_of(b, i, pt, x):
        return pt[b * n_pages + (npb - 1 - i) * pp + x]

    def kv_spec(x):
        return pl.BlockSpec((None, None, df, page), lambda b, i, pt: (layer, page_of(b, i, pt, x), 0, 0))

    def lf_spec(x):
        return pl.BlockSpec((None, None, hf, page), lambda b, i, pt: (layer, page_of(b, i, pt, x), 0, 0))

    grid_spec = pltpu.PrefetchScalarGridSpec(
        num_scalar_prefetch=1,
        grid=(n_seq, npb),
        in_specs=[seq_blk(qb), seq_blk(kb), seq_blk(vb),
                  pl.BlockSpec((None, rows, 1), lambda b, i, pt: (b, 0, 0)),
                  pl.BlockSpec((None, hf, nq, nq), lambda b, i, pt: (b, 0, 0, 0))]
                 + [lf_spec(x) for x in range(pp)] + [kv_spec(x) for x in range(pp)] * 2,
        out_specs=pl.BlockSpec((None, nq, df), lambda b, i, pt: (b, 0, 0)),
        scratch_shapes=[pltpu.VMEM((hf // gh, gh * nq, MXU_DEPTH), BF16), pltpu.VMEM((rows, 1), F32),
                        pltpu.VMEM((rows, 1), F32), pltpu.VMEM((hf // gh, gh * nq, MXU_DEPTH), F32),
                        pltpu.VMEM((hf, page), F32)],
    )
    out = pl.pallas_call(
        functools.partial(_fox_sample_kernel, pp=pp, page=page, nq=nq, hf=hf, scale=HEAD_DIM ** -0.5),
        grid_spec=grid_spec,
        out_shape=jax.ShapeDtypeStruct((n_seq, nq, df), BF16),
        compiler_params=_cparams(("parallel", "arbitrary")),
        name="fox_sample",
    )(page_table_flat, p3, p3, p3, cn, cn_row, *([cache_lft] * pp), *([cache_k] * pp), *([cache_v] * pp))
    return out.reshape(n_seq * nq, df)


def _gelu_tanh(x):
    return 0.5 * x * (1.0 + jnp.tanh(0.7978845608028654 * (x + 0.044715 * (x * x * x))))


def _ffn_in_kernel(*refs, tm, tn, seq_rows, pad, sample):
    if sample:
        x_ref, wu_ref, wg_ref, cw_ref, cb_ref, p1_ref, p2_ref, h_ref, u_ref = refs
    else:
        x_ref, wu_ref, wg_ref, cw_ref, cb_ref, h_ref, tail_ref, carry = refs
    i = pl.program_id(0)
    j = pl.program_id(1)
    x = x_ref[...]
    u = _dot(x, wu_ref[...])
    gate = _dot(x, wg_ref[...])
    row = lax.broadcasted_iota(jnp.int32, (tm, 1), 0)
    if sample:
        t = _imod(row, seq_rows)
        u1 = jnp.where(t == 0, p1_ref[...], pltpu.roll(u, 1, 0))
        u2 = jnp.where(t < 2, p2_ref[...], pltpu.roll(u, 2, 0))
        u_ref[...] = u
    else:
        @pl.when(i == 0)
        def _():
            carry[:, pl.ds(pl.multiple_of(j * tn, tn), tn)] = jnp.zeros((8, tn), F32)

        pos = (i * tm) % seq_rows + row
        u = jnp.where(pos >= pad, u, 0.0)
        prev = carry[:, pl.ds(pl.multiple_of(j * tn, tn), tn)]
        u1 = jnp.where(row == 0, prev[7:8, :], pltpu.roll(u, 1, 0))
        u2 = jnp.where(row == 0, prev[6:7, :], jnp.where(row == 1, prev[7:8, :], pltpu.roll(u, 2, 0)))
        tail = u[tm - 8:tm, :]
        carry[:, pl.ds(pl.multiple_of(j * tn, tn), tn)] = tail
        tail_ref[...] = tail
    c = cb_ref[...] + cw_ref[0:1, :] * u2 + cw_ref[1:2, :] * u1 + cw_ref[2:3, :] * u
    h_ref[...] = (_gelu_tanh(c) * gate).astype(BF16)


def _ffn_in(x1b, w_ffn_in_b, conv_w, conv_b, p1, p2, *, seq_rows, pad, sample):
    m, d = x1b.shape
    dff = conv_w.shape[-1]
    tn = _pick(dff, (256, 128))
    nj = dff // tn
    tm = m if sample else _pick(seq_rows, (528, 512, 384, 256, 128, 64, 32, 16, 8))
    ni = m // tm
    in_specs = [pl.BlockSpec((tm, d), lambda i, j: (i, 0)),
                pl.BlockSpec((d, tn), lambda i, j: (0, j)),
                pl.BlockSpec((d, tn), lambda i, j: (0, j + nj)),
                pl.BlockSpec((3, tn), lambda i, j: (0, j)),
                pl.BlockSpec((1, tn), lambda i, j: (0, j))]
    args = [x1b, w_ffn_in_b, w_ffn_in_b, conv_w, conv_b.reshape(1, dff)]
    tile = pl.BlockSpec((tm, tn), lambda i, j: (i, j))
    if sample:
        in_specs += [tile, tile]
        args += [p1, p2]
        out_specs = [tile, tile]
        out_shape = [jax.ShapeDtypeStruct((m, dff), BF16), jax.ShapeDtypeStruct((m, dff), F32)]
        scratch = []
    else:
        out_specs = [tile, pl.BlockSpec((8, tn), lambda i, j: (i, j))]
        out_shape = [jax.ShapeDtypeStruct((m, dff), BF16), jax.ShapeDtypeStruct((ni * 8, dff), F32)]
        scratch = [pltpu.VMEM((8, dff), F32)]
    return pl.pallas_call(
        functools.partial(_ffn_in_kernel, tm=tm, tn=tn, seq_rows=seq_rows, pad=pad, sample=sample),
        grid=(ni, nj),
        in_specs=in_specs,
        out_specs=out_specs,
        out_shape=out_shape,
        scratch_shapes=scratch,
        compiler_params=_cparams(("arbitrary", "arbitrary")),
        name="ffn_in_sample" if sample else "ffn_in_prompt",
    )(*args)


def _layer_params(lw, d_model):
    (w_in, b_f, mu_shift, w0, w_decay_up, a0, w_iclr_up, w_gate_up, k_k, k_a, r_k,
     gn_g, gn_b, w_out, ln1_g, ln1_b, w_ffn_in, conv_w, conv_b, w_ffn_out, ln2_g, ln2_b) = lw
    dr = w0.shape[-1]
    hf = b_f.shape[-1]
    df = hf * HEAD_DIM
    d_dec, d_icl, d_gate = w_decay_up.shape[0], w_iclr_up.shape[0], w_gate_up.shape[0]
    rw_cols = 3 * dr + d_dec + d_icl + d_gate
    n_lora = d_dec + d_icl + d_gate
    lora_w = -(-(n_lora + hf) // (4 * LANES)) * (4 * LANES)
    assert (3 * dr + 3 * df) % lora_w == 0 and dr % LANES == 0 and df % LANES == 0
    fox0 = rw_cols
    w_perm = jnp.concatenate(
        [w_in[:, :3 * dr], w_in[:, fox0:fox0 + 3 * df], w_in[:, 3 * dr:rw_cols],
         w_in[:, fox0 + 3 * df:], jnp.zeros((d_model, lora_w - n_lora - hf), w_in.dtype)], axis=1).astype(BF16)

    def pad_rows(w, off):
        return jnp.zeros((lora_w, dr), F32).at[off:off + w.shape[0]].set(w).astype(BF16)

    prm = dict(
        dr=dr, df=df, lora_w=lora_w, d_dec=d_dec, d_icl=d_icl, f_off=n_lora, rw_cols=rw_cols,
        mu_rkv=mu_shift[:3 * dr].reshape(1, -1),
        mu_lora=jnp.zeros((1, lora_w), F32).at[0, :n_lora].set(mu_shift[3 * dr:]),
        wd=pad_rows(w_decay_up, 0), wa=pad_rows(w_iclr_up, d_dec), wg=pad_rows(w_gate_up, d_dec + d_icl),
        w0=w0.reshape(1, dr), a0=a0.reshape(1, dr), k_k=k_k.reshape(1, dr), k_a=k_a.reshape(1, dr),
        r_k=r_k.reshape(1, dr), w_perm=w_perm, w_out=w_out.astype(BF16), w_ffn_in=w_ffn_in.astype(BF16),
        w_ffn_out=w_ffn_out.astype(BF16), b_f=b_f, gn_g=gn_g, gn_b=gn_b, ln1=(ln1_g, ln1_b),
        ln2=(ln2_g, ln2_b), conv_w=conv_w, conv_b=conv_b)
    return prm


def _mm_tiles(m, n, seq_rows):
    tm = _pick(seq_rows, (1056, 1024, 768, 512, 384, 256, 128, 64, 32, 16, 8)) if m > 512 else m
    tn = _pick(n, (640, 512, 256, 128))
    return tm, tn


def _permute_rw_row(row, prm):
    dr = prm["dr"]
    n_lora = prm["f_off"]
    lora = jnp.zeros(row.shape[:-1] + (prm["lora_w"],), F32).at[..., :n_lora].set(row[..., 3 * dr:])
    return row[..., :3 * dr], lora


def _run_group(x2, ln0, prm, alpha, *, n_seq, seq_rows, pad, skip, sample, shift_prev, s0, ffn_prev, attend):
    ln0_g, ln0_b = ln0
    m, d = x2.shape
    dr, df = prm["dr"], prm["df"]
    h_b = _ln0(x2, ln0_g, ln0_b)
    tm, tn = _mm_tiles(m, prm["w_perm"].shape[1], seq_rows)
    p = _matmul(h_b, prm["w_perm"], tm, tn, "in_proj")

    if sample:
        sp_rkv, sp_lora = _permute_rw_row(shift_prev, prm)
        expand = lambda a: jnp.zeros((n_seq, seq_rows, a.shape[-1]), F32).at[:, 0].set(a).reshape(m, -1)
        sp_rkv, sp_lora = expand(sp_rkv), expand(sp_lora)
    else:
        sp_rkv = sp_lora = None
    r, w, k, v, kk, b, g, bonus = _rwkv_prep(p, sp_rkv, sp_lora, prm, seq_rows=seq_rows, pad=pad, sample=sample)
    o_scan, s_last = _rwkv_scan(r, w, k, kk, b, v, s0, n_seq, seq_rows)
    o_rw = _rwkv_post(o_scan, g, bonus, prm["gn_g"], prm["gn_b"])

    lf, c = _logf_cumsum(p, prm["b_f"], prm, n_seq, seq_rows, pad)
    o_fox = attend(p, c)

    mix_in = jnp.concatenate([o_rw, o_fox], axis=1)
    tm, tn = _mm_tiles(m, d, seq_rows)
    mix = _matmul(mix_in, prm["w_out"], tm, tn, "out_proj")
    x1, x1b = _ln1(x2, mix, ln0_g, ln0_b, *prm["ln1"], alpha)

    if sample:
        p1 = jnp.zeros((n_seq, seq_rows, ffn_prev.shape[-1]), F32).at[:, 0].set(ffn_prev[:, 1])
        p2 = p1.at[:, 0].set(ffn_prev[:, 0]).at[:, 1].set(ffn_prev[:, 1])
        p1, p2 = p1.reshape(m, -1), p2.reshape(m, -1)
    else:
        p1 = p2 = None
    hid, u_aux = _ffn_in(x1b, prm["w_ffn_in"], prm["conv_w"], prm["conv_b"], p1, p2,
                         seq_rows=seq_rows, pad=pad, sample=sample)
    dff = hid.shape[1]
    tm_o = _pick(seq_rows, (528, 512, 384, 256, 128, 64, 32, 16, 8)) if m > 512 else m
    ffn = _matmul(hid, prm["w_ffn_out"], tm_o, _pick(d, (256, 128)), "ffn_out")
    y = _ln2(x1, ffn, *prm["ln2"], alpha, n_seq, seq_rows, skip)

    p3 = p.reshape(n_seq, seq_rows, -1)
    hf = df // HEAD_DIM
    k_new = p3[:, pad:, 3 * dr + df:3 * dr + 2 * df].reshape(n_seq, seq_rows - pad, hf, HEAD_DIM)
    v_new = p3[:, pad:, 3 * dr + 2 * df:3 * dr + 3 * df].reshape(n_seq, seq_rows - pad, hf, HEAD_DIM)
    last = p3[:, -1]
    shift_last = jnp.concatenate([last[:, :3 * dr], last[:, 3 * dr + 3 * df:3 * dr + 3 * df + prm["f_off"]]], axis=1)
    if sample:
        conv_state = u_aux.reshape(n_seq, seq_rows, dff)[:, -2:]
    else:
        tiles_per_seq = u_aux.shape[0] // 8 // n_seq
        conv_state = u_aux.reshape(n_seq, tiles_per_seq, 8, dff)[:, -1, -2:]
    return y, (k_new, v_new, lf[:, pad:], s_last, shift_last, conv_state)


def kernel(x_prompt, x_sample, cache_k, cache_v, cache_logf, state_rwkv, state_shift, state_ffn_conv,
           page_table, meta_tokens, ln0_g, ln0_b, w_in, b_f, mu_shift, w0, w_decay_up, a0, w_iclr_up,
           w_gate_up, k_k, k_a, r_k, gn_g, gn_b, w_out, ln1_g, ln1_b, w_ffn_in, conv_w, conv_b,
           w_ffn_out, ln2_g, ln2_b):
    depth = w_in.shape[0]
    assert depth == 1, "the token layout between layers is only wired for a single layer"
    b, seq, d = x_prompt.shape
    db, nq, _ = x_sample.shape
    n_meta = meta_tokens.shape[0]
    alpha = (2 * depth) ** 0.25
    n_pool, page = cache_k.shape[1:3]
    n_pages = page_table.shape[1]
    h_rw = r_k.shape[1]

    t_real = seq + n_meta
    pad = (-t_real) % Q_BLOCK
    t_pad = t_real + pad
    meta = jnp.broadcast_to(meta_tokens[None].astype(x_prompt.dtype), (b, n_meta, d))
    xp = jnp.concatenate([jnp.zeros((b, pad, d), x_prompt.dtype), meta, x_prompt], axis=1).reshape(b * t_pad, d)
    xs = x_sample.reshape(db * nq, d)
    pt_flat = page_table.reshape(-1).astype(jnp.int32)

    outs_p, outs_s = [], []
    for l in range(depth):
        lw = tuple(wt[l] for wt in (w_in, b_f, mu_shift, w0, w_decay_up, a0, w_iclr_up, w_gate_up, k_k, k_a,
                                     r_k, gn_g, gn_b, w_out, ln1_g, ln1_b, w_ffn_in, conv_w, conv_b,
                                     w_ffn_out, ln2_g, ln2_b))
        prm = _layer_params(lw, d)
        hf = prm["df"] // HEAD_DIM

        attend_p = lambda p, c: _fox_prompt(p, c, prm, b, t_pad, pad)
        y_p, st_p = _run_group(xp, (ln0_g, ln0_b), prm, alpha, n_seq=b, seq_rows=t_pad, pad=pad, skip=pad + n_meta, sample=False,
                               shift_prev=None, s0=jnp.zeros((b, h_rw, HEAD_DIM, HEAD_DIM), F32),
                               ffn_prev=None, attend=attend_p)

        cache_kt = cache_k.transpose(0, 1, 3, 4, 2).reshape(depth, n_pool, hf * HEAD_DIM, page)
        cache_vt = cache_v.transpose(0, 1, 3, 4, 2).reshape(depth, n_pool, hf * HEAD_DIM, page)
        cache_lft = cache_logf.astype(F32).transpose(0, 1, 3, 2)
        attend_s = lambda p, c: _fox_sample(p, c, cache_kt, cache_vt, cache_lft, pt_flat, prm, db, nq, n_pages, l)
        y_s, st_s = _run_group(xs, (ln0_g, ln0_b), prm, alpha, n_seq=db, seq_rows=nq, pad=0, skip=0, sample=True,
                               shift_prev=state_shift[l], s0=state_rwkv[l], ffn_prev=state_ffn_conv[l],
                               attend=attend_s)
        outs_p.append(st_p)
        outs_s.append(st_s)

    k_p, v_p, lf_p, rw_p, sh_p, cv_p = (jnp.stack([o[i] for o in outs_p]) for i in range(6))
    k_s, v_s, lf_s, rw_s, sh_s, cv_s = (jnp.stack([o[i] for o in outs_s]) for i in range(6))
    y_prompt = y_p
    y_sample = y_s.reshape(db, nq, d)
    return (y_prompt, y_sample, k_p, v_p, lf_p, rw_p, sh_p, cv_p, k_s, v_s, lf_s, rw_s, sh_s, cv_s)
```

```python
import functools

import jax
import jax.numpy as jnp
import numpy as np
from jax import lax
from jax.experimental import pallas as pl
from jax.experimental.pallas import tpu as pltpu

F32 = jnp.float32
BF16 = jnp.bfloat16

HEAD_DIM = 64
LANES = 128
SUBLANES = 8
MXU_DEPTH = 256
Q_BLOCK = 128
LN_EPS = 1e-5
GN_EPS = 64e-5
NEG_INF = -1e30
VMEM_LIMIT = 56 * 1024 * 1024


def _cparams(sem):
    return pltpu.CompilerParams(dimension_semantics=sem, vmem_limit_bytes=VMEM_LIMIT)


def _pick(n, candidates):
    for c in candidates:
        if n % c == 0:
            return c
    return n


def _split_bf16(x):
    hi = x.astype(BF16)
    lo = (x - hi.astype(F32)).astype(BF16)
    return hi, lo


def _dot(a, b):
    return jnp.dot(a, b, preferred_element_type=F32)


def _dot_nt(a, b):
    return lax.dot_general(a, b, (((1,), (1,)), ((), ())), preferred_element_type=F32)


def _dot_x01(x, m01):
    hi, lo = _split_bf16(x)
    return _dot(hi, m01) + _dot(lo, m01)


def _dot_01x(m01, x):
    hi, lo = _split_bf16(x)
    return _dot(m01, hi) + _dot(m01, lo)


def _idiv(x, c):
    return lax.shift_right_logical(x, c.bit_length() - 1) if c & (c - 1) == 0 else x // c


def _imod(x, c):
    return (x & (c - 1)) if c & (c - 1) == 0 else x % c


def _head_select(n_cols, n_heads):
    c = lax.broadcasted_iota(jnp.int32, (n_cols, n_heads), 0)
    h = lax.broadcasted_iota(jnp.int32, (n_cols, n_heads), 1)
    return jnp.where(_idiv(c, HEAD_DIM) == h, 1.0, 0.0).astype(BF16)


def _head_expand(n_heads, n_cols):
    h = lax.broadcasted_iota(jnp.int32, (n_heads, n_cols), 0)
    c = lax.broadcasted_iota(jnp.int32, (n_heads, n_cols), 1)
    return jnp.where(_idiv(c, HEAD_DIM) == h, 1.0, 0.0).astype(BF16)


def _head_sum_bcast(x, sel, exp):
    return _dot_x01(_dot_x01(x, sel), exp)


def _softplus(z):
    return jnp.maximum(z, 0.0) + jnp.log1p(jnp.exp(-jnp.abs(z)))


def _sigmoid(z):
    return 1.0 / (1.0 + jnp.exp(-z))


def _ln_rows(x, g, b):
    mu = jnp.mean(x, axis=-1, keepdims=True)
    xc = x - mu
    var = jnp.mean(xc * xc, axis=-1, keepdims=True)
    return xc * lax.rsqrt(var + LN_EPS) * g + b


def _ln0_kernel(x_ref, g_ref, b_ref, o_ref):
    o_ref[...] = _ln_rows(x_ref[...], g_ref[...], b_ref[...]).astype(o_ref.dtype)


def _ln0(x2, g, b):
    m, d = x2.shape
    tm = _pick(m, (528, 512, 384, 256, 128, 64, 32, 16, 8))
    return pl.pallas_call(
        _ln0_kernel,
        grid=(m // tm,),
        in_specs=[pl.BlockSpec((tm, d), lambda i: (i, 0)),
                  pl.BlockSpec((1, d), lambda i: (0, 0)),
                  pl.BlockSpec((1, d), lambda i: (0, 0))],
        out_specs=pl.BlockSpec((tm, d), lambda i: (i, 0)),
        out_shape=jax.ShapeDtypeStruct((m, d), BF16),
        compiler_params=_cparams(("parallel",)),
        name="ln0",
    )(x2, g.reshape(1, d), b.reshape(1, d))


def _ln1_kernel(x_ref, mix_ref, g0_ref, b0_ref, g_ref, b_ref, of_ref, ob_ref, *, alpha):
    h = _ln_rows(x_ref[...], g0_ref[...], b0_ref[...])
    x1 = _ln_rows(alpha * h + mix_ref[...], g_ref[...], b_ref[...])
    of_ref[...] = x1
    ob_ref[...] = x1.astype(BF16)


def _ln1(x2, mix, g0, b0, g, b, alpha):
    m, d = x2.shape
    tm = _pick(m, (264, 256, 128, 64, 32, 16, 8))
    row = pl.BlockSpec((tm, d), lambda i: (i, 0))
    vec = pl.BlockSpec((1, d), lambda i: (0, 0))
    return pl.pallas_call(
        functools.partial(_ln1_kernel, alpha=alpha),
        grid=(m // tm,),
        in_specs=[row, row, vec, vec, vec, vec],
        out_specs=[row, row],
        out_shape=[jax.ShapeDtypeStruct((m, d), F32), jax.ShapeDtypeStruct((m, d), BF16)],
        compiler_params=_cparams(("parallel",)),
        name="ln1",
    )(x2, mix, g0.reshape(1, d), b0.reshape(1, d), g.reshape(1, d), b.reshape(1, d))


def _ln2_kernel(x1_ref, ffn_ref, g_ref, b_ref, o_ref, *, alpha):
    o_ref[...] = _ln_rows(alpha * x1_ref[...] + ffn_ref[...], g_ref[...], b_ref[...])


def _ln2(x1, ffn, g, b, alpha, n_seq, seq_rows, skip_rows):
    d = x1.shape[-1]
    out_rows = seq_rows - skip_rows
    tb = next(c for c in (256, 128, 64, 32, 16, 8) if skip_rows % c == 0 and out_rows % c == 0)
    skip = skip_rows // tb
    x3 = x1.reshape(n_seq, seq_rows, d)
    f3 = ffn.reshape(n_seq, seq_rows, d)
    row_in = pl.BlockSpec((None, tb, d), lambda s, i: (s, i + skip, 0))
    vec = pl.BlockSpec((1, d), lambda s, i: (0, 0))
    return pl.pallas_call(
        functools.partial(_ln2_kernel, alpha=alpha),
        grid=(n_seq, out_rows // tb),
        in_specs=[row_in, row_in, vec, vec],
        out_specs=pl.BlockSpec((None, tb, d), lambda s, i: (s, i, 0)),
        out_shape=jax.ShapeDtypeStruct((n_seq, out_rows, d), F32),
        compiler_params=_cparams(("parallel", "parallel")),
        name="ln2",
    )(x3, f3, g.reshape(1, d), b.reshape(1, d))


def _mm_kernel(x_ref, w_ref, o_ref):
    o_ref[...] = _dot(x_ref[...], w_ref[...])


def _matmul(x, w, tm, tn, name):
    m, k = x.shape
    n = w.shape[1]
    return pl.pallas_call(
        _mm_kernel,
        grid=(m // tm, n // tn),
        in_specs=[pl.BlockSpec((tm, k), lambda i, j: (i, 0)),
                  pl.BlockSpec((k, tn), lambda i, j: (0, j))],
        out_specs=pl.BlockSpec((tm, tn), lambda i, j: (i, j)),
        out_shape=jax.ShapeDtypeStruct((m, n), F32),
        compiler_params=_cparams(("parallel", "arbitrary")),
        name=name,
    )(x, w)


CHAINS = LANES // 2
KH = HEAD_DIM // 2
KQ = KH // 2
VQ = HEAD_DIM // 4


def _rwkv_col_orders(n_heads):
    assert 4 * n_heads == LANES
    c = np.arange(n_heads * HEAD_DIM)
    h, rest = c % n_heads, c // n_heads
    pk = h * HEAD_DIM + ((rest // 2) % 2) * KH + (rest % 2) * KQ + rest // 4
    pv = h * HEAD_DIM + (rest % 4) * VQ + rest // 4
    return pk, pv


def _head_select_minor(n_cols, n_heads):
    c = lax.broadcasted_iota(jnp.int32, (n_cols, n_heads), 0)
    h = lax.broadcasted_iota(jnp.int32, (n_cols, n_heads), 1)
    return jnp.where(_imod(c, n_heads) == h, 1.0, 0.0).astype(BF16)


def _head_expand_minor(n_heads, n_cols):
    h = lax.broadcasted_iota(jnp.int32, (n_heads, n_cols), 0)
    c = lax.broadcasted_iota(jnp.int32, (n_heads, n_cols), 1)
    return jnp.where(_imod(c, n_heads) == h, 1.0, 0.0).astype(BF16)


def _pair_to_chain_k(x0, x1):
    tm = x0.shape[0]
    a = x0.reshape(tm, KQ, LANES)
    b = x1.reshape(tm, KQ, LANES)
    lane = lax.broadcasted_iota(jnp.int32, (tm, KQ, LANES), 2)
    seq0 = _imod(lane, CHAINS) < CHAINS // 2
    t0 = jnp.where(seq0, a, pltpu.roll(b, CHAINS // 2, 2))
    t1 = jnp.where(seq0, pltpu.roll(a, LANES - CHAINS // 2, 2), b)
    return jnp.concatenate([t0, t1], axis=1)


def _pair_to_chain_v(x0, x1):
    tm = x0.shape[0]
    a = x0.reshape(tm, VQ, LANES)
    b = x1.reshape(tm, VQ, LANES)
    q = CHAINS // 2
    grp = _idiv(lax.broadcasted_iota(jnp.int32, (tm, VQ, LANES), 2), q)
    outs = []
    for j in range(4):
        pa = jnp.where(grp == j, a, 0.0)
        pb = jnp.where(grp == j, b, 0.0)
        sa, sb = (LANES - q * j) % LANES, (q - q * j) % LANES
        y = (pltpu.roll(pa, sa, 2) if sa else pa) + (pltpu.roll(pb, sb, 2) if sb else pb)
        outs.append(y + pltpu.roll(y, CHAINS, 2))
    return jnp.concatenate(outs, axis=1)


def _chain_to_pair_v(o):
    tm = o.shape[0]
    q = CHAINS // 2
    grp = _idiv(lax.broadcasted_iota(jnp.int32, (tm, VQ, LANES), 2), q)
    res = []
    for s in range(2):
        acc = jnp.zeros((tm, VQ, LANES), F32)
        for j in range(4):
            piece = o[:, j * VQ:(j + 1) * VQ, :]
            sh = (q * j - q * s) % LANES
            acc = jnp.where(grp == j, pltpu.roll(piece, sh, 2) if sh else piece, acc)
        res.append(acc.reshape(tm, VQ * LANES))
    return res


def _rwkv_prep_kernel(*refs, pad, tm, dr, n_heads, sample):
    if sample:
        (rkv_ref, lora_ref, sp_rkv_ref, sp_lora_ref, mu_rkv_ref, mu_lora_ref, wd_ref, wa_ref, wg_ref,
         w0_ref, a0_ref, kk_s_ref, ka_s_ref, rk_ref,
         r_o, w_o, k_o, v_o, kk_o, b_o, g_o, bon_o) = refs
    else:
        (rkv_ref, lora_ref, mu_rkv_ref, mu_lora_ref, wd_ref, wa_ref, wg_ref,
         w0_ref, a0_ref, kk_s_ref, ka_s_ref, rk_ref,
         r_o, w_o, k_o, v_o, kk_o, b_o, g_o, bon_o, carry_rkv, carry_lora) = refs
    i = pl.program_id(1)
    row = lax.broadcasted_iota(jnp.int32, (tm, 1), 0)
    sel = _head_select_minor(dr, n_heads)
    exp = _head_expand_minor(n_heads, dr)

    if not sample:
        @pl.when(i == 0)
        def _():
            carry_rkv[...] = jnp.zeros_like(carry_rkv)
            carry_lora[...] = jnp.zeros_like(carry_lora)

    per_seq = []
    for s in range(2):
        p_rkv = rkv_ref[s]
        p_lora = lora_ref[s]
        if sample:
            prev_rkv = jnp.where(row == 0, sp_rkv_ref[s], pltpu.roll(p_rkv, 1, 0))
            prev_lora = jnp.where(row == 0, sp_lora_ref[s], pltpu.roll(p_lora, 1, 0))
        else:
            pos = i * tm + row
            real = pos >= pad
            p_rkv = jnp.where(real, p_rkv, 0.0)
            p_lora = jnp.where(real, p_lora, 0.0)
            prev_rkv = jnp.where(row == 0, carry_rkv[s, 0:1, :], pltpu.roll(p_rkv, 1, 0))
            prev_lora = jnp.where(row == 0, carry_lora[s, 0:1, :], pltpu.roll(p_lora, 1, 0))
            prev_rkv = jnp.where(pos > pad, prev_rkv, 0.0)
            prev_lora = jnp.where(pos > pad, prev_lora, 0.0)
            carry_rkv[s, 0:1, :] = p_rkv[tm - 1:tm, :]
            carry_lora[s, 0:1, :] = p_lora[tm - 1:tm, :]

        xs_rkv = p_rkv + (prev_rkv - p_rkv) * mu_rkv_ref[...]
        xl = p_lora + (prev_lora - p_lora) * mu_lora_ref[...]
        r = xs_rkv[:, 0:dr]
        k = xs_rkv[:, dr:2 * dr]
        v = xs_rkv[:, 2 * dr:3 * dr]

        dec = _dot(jnp.tanh(xl).astype(BF16), wd_ref[...])
        a_pre = _dot(xl.astype(BF16), wa_ref[...])
        g = _dot(_sigmoid(xl).astype(BF16), wg_ref[...])
        w_log = -_softplus(-(w0_ref[...] + dec)) - 0.5
        decay = jnp.exp(-jnp.exp(w_log))
        a = _sigmoid(a0_ref[...] + a_pre)

        kk = k * kk_s_ref[...]
        nrm = jnp.sqrt(_head_sum_bcast(kk * kk, sel, exp))
        kk = kk / jnp.maximum(nrm, 1e-12)
        k_mod = k * (1.0 + (a - 1.0) * ka_s_ref[...])
        g_o[s] = g
        bon_o[s] = _head_sum_bcast(r * k_mod * rk_ref[...], sel, exp) * v
        per_seq.append((r, decay, k_mod, kk, kk * a, v))

    for idx, out in enumerate((r_o, w_o, k_o, kk_o, b_o)):
        out[...] = _pair_to_chain_k(per_seq[0][idx], per_seq[1][idx])
    v_o[...] = _pair_to_chain_v(per_seq[0][5], per_seq[1][5])


def _rwkv_prep(p3, sp_rkv, sp_lora, prm, *, pad, sample):
    n_seq, t, _ = p3.shape
    assert n_seq % 2 == 0
    n_pairs = n_seq // 2
    dr = prm["dr"]
    dl = prm["lora_w"]
    tm = t if sample else _pick(t, (64, 32, 16, 8))
    lora_blk = (3 * dr + 3 * prm["df"]) // dl
    full = lambda a: pl.BlockSpec(a.shape, lambda p, i: (0,) * a.ndim)
    consts = [prm["mu_rkv"], prm["mu_lora"], prm["wd"], prm["wa"], prm["wg"], prm["w0"], prm["a0"],
              prm["k_k"], prm["k_a"], prm["r_k"]]
    in_specs = [pl.BlockSpec((2, tm, 3 * dr), lambda p, i: (p, i, 0)),
                pl.BlockSpec((2, tm, dl), lambda p, i: (p, i, lora_blk))]
    args = [p3, p3]
    if sample:
        in_specs += [pl.BlockSpec((2, tm, 3 * dr), lambda p, i: (p, i, 0)),
                     pl.BlockSpec((2, tm, dl), lambda p, i: (p, i, 0))]
        args += [sp_rkv, sp_lora]
    in_specs += [full(c) for c in consts]
    args += consts
    kspec = pl.BlockSpec((tm, KH, LANES), lambda p, i: (i, 0, p))
    vspec = pl.BlockSpec((tm, HEAD_DIM, LANES), lambda p, i: (i, 0, p))
    rspec = pl.BlockSpec((2, tm, dr), lambda p, i: (p, i, 0))
    kshape = jax.ShapeDtypeStruct((t, KH, n_pairs * LANES), F32)
    vshape = jax.ShapeDtypeStruct((t, HEAD_DIM, n_pairs * LANES), F32)
    rshape = jax.ShapeDtypeStruct((n_seq, t, dr), F32)
    scratch = [] if sample else [pltpu.VMEM((2, 8, 3 * dr), F32), pltpu.VMEM((2, 8, dl), F32)]
    return pl.pallas_call(
        functools.partial(_rwkv_prep_kernel, pad=pad, tm=tm, dr=dr, n_heads=dr // HEAD_DIM, sample=sample),
        grid=(n_pairs, t // tm),
        in_specs=in_specs,
        out_specs=[kspec, kspec, kspec, vspec, kspec, kspec, rspec, rspec],
        out_shape=[kshape, kshape, kshape, vshape, kshape, kshape, rshape, rshape],
        scratch_shapes=scratch,
        compiler_params=_cparams(("parallel", "arbitrary")),
        name="rwkv_prep_sample" if sample else "rwkv_prep_prompt",
    )(*args)


def _rwkv_scan_kernel(r_ref, w_ref, k_ref, kk_ref, b_ref, v_ref, s0_ref, o_ref, sf_ref, s_scr, *, tb):
    t_blk = pl.program_id(1)

    @pl.when(t_blk == 0)
    def _():
        s_scr[...] = s0_ref[...]

    def step(t, carry):
        r = r_ref[t]
        w = w_ref[t]
        k = k_ref[t]
        kk = kk_ref[t]
        b = b_ref[t]
        for vi in range(HEAD_DIM):
            sv = s_scr[vi]
            skk = jnp.sum(sv * kk, axis=0, keepdims=True)
            skk = skk + pltpu.roll(skk, CHAINS, 1)
            vrow = v_ref[t, pl.ds(vi, 1), :]
            sv = sv * w - skk * b + vrow * k
            s_scr[vi] = sv
            o = jnp.sum(sv * r, axis=0, keepdims=True)
            o_ref[t, pl.ds(vi, 1), :] = o + pltpu.roll(o, CHAINS, 1)
        return carry

    lax.fori_loop(0, tb, step, 0)

    @pl.when(t_blk == pl.num_programs(1) - 1)
    def _():
        sf_ref[...] = s_scr[...]


def _state_to_chain_layout(s):
    n_seq, h = s.shape[:2]
    g = (n_seq * h) // CHAINS
    s = s.reshape(g, CHAINS, HEAD_DIM, 2, KH).transpose(2, 4, 0, 3, 1)
    return s.reshape(HEAD_DIM, KH, g * LANES)


def _state_from_chain_layout(s, n_seq, h):
    g = (n_seq * h) // CHAINS
    s = s.reshape(HEAD_DIM, KH, g, 2, CHAINS).transpose(2, 4, 0, 3, 1)
    return s.reshape(n_seq, h, HEAD_DIM, HEAD_DIM)


def _rwkv_scan(r, w, k, kk, b, v, s0):
    n_seq, h = s0.shape[:2]
    t = r.shape[0]
    g = (n_seq * h) // CHAINS
    tb = _pick(t, (64, 32, 16, 8, 4, 2, 1))
    ins = [r, w, k, kk, b]
    vin = v
    sin = _state_to_chain_layout(s0)
    kspec = pl.BlockSpec((tb, KH, LANES), lambda gi, ti: (ti, 0, gi))
    vspec = pl.BlockSpec((tb, HEAD_DIM, LANES), lambda gi, ti: (ti, 0, gi))
    sspec = pl.BlockSpec((HEAD_DIM, KH, LANES), lambda gi, ti: (0, 0, gi))
    o, sf = pl.pallas_call(
        functools.partial(_rwkv_scan_kernel, tb=tb),
        grid=(g, t // tb),
        in_specs=[kspec] * 5 + [vspec, sspec],
        out_specs=[vspec, sspec],
        out_shape=[jax.ShapeDtypeStruct((t, HEAD_DIM, g * LANES), F32),
                   jax.ShapeDtypeStruct((HEAD_DIM, KH, g * LANES), F32)],
        scratch_shapes=[pltpu.VMEM((HEAD_DIM, KH, LANES), F32)],
        compiler_params=_cparams(("parallel", "arbitrary")),
        name="rwkv_scan",
    )(*ins, vin, sin)
    return o, _state_from_chain_layout(sf, n_seq, h)


def _rwkv_post_kernel(o_ref, g_ref, bon_ref, gg_ref, gb_ref, out_ref, *, dr):
    n_heads = dr // HEAD_DIM
    sel = _head_select_minor(dr, n_heads)
    exp = _head_expand_minor(n_heads, dr)
    for s, o in enumerate(_chain_to_pair_v(o_ref[...])):
        mu = _head_sum_bcast(o, sel, exp) * (1.0 / HEAD_DIM)
        oc = o - mu
        var = _head_sum_bcast(oc * oc, sel, exp) * (1.0 / HEAD_DIM)
        y = oc * lax.rsqrt(var + GN_EPS) * gg_ref[...] + gb_ref[...]
        out_ref[s] = ((y + bon_ref[s]) * g_ref[s]).astype(BF16)


def _rwkv_post(o, g, bonus, gn_g, gn_b):
    n_seq, t, dr = g.shape
    tm = t if t <= 64 else _pick(t, (64, 32, 16, 8))
    row = pl.BlockSpec((2, tm, dr), lambda p, i: (p, i, 0))
    vec = pl.BlockSpec((1, dr), lambda p, i: (0, 0))
    return pl.pallas_call(
        functools.partial(_rwkv_post_kernel, dr=dr),
        grid=(n_seq // 2, t // tm),
        in_specs=[pl.BlockSpec((tm, HEAD_DIM, LANES), lambda p, i: (i, 0, p)), row, row, vec, vec],
        out_specs=row,
        out_shape=jax.ShapeDtypeStruct((n_seq, t, dr), BF16),
        compiler_params=_cparams(("parallel", "parallel")),
        name="rwkv_post",
    )(o, g, bonus, gn_g.reshape(1, dr), gn_b.reshape(1, dr))


def _logf_kernel(lora_ref, bf_ref, lf_ref, c_ref, carry, *, tb, pad, f_off, n_heads):
    i = pl.program_id(1)

    @pl.when(i == 0)
    def _():
        carry[...] = jnp.zeros_like(carry)

    z = lora_ref[:, f_off:f_off + n_heads] + bf_ref[...]
    lf = -_softplus(-z)
    pos = i * tb + lax.broadcasted_iota(jnp.int32, (tb, 1), 0)
    lf = jnp.where(pos >= pad, lf, 0.0)
    rr = lax.broadcasted_iota(jnp.int32, (tb, tb), 0)
    cc = lax.broadcasted_iota(jnp.int32, (tb, tb), 1)
    tril = jnp.where(cc <= rr, 1.0, 0.0).astype(BF16)
    c = _dot_01x(tril, lf) + carry[0:1, :]
    lf_ref[...] = lf
    c_ref[...] = c
    carry[0:1, :] = c[tb - 1:tb, :]


def _logf_cumsum(p, b_f, prm, n_seq, seq_rows, pad):
    dl = prm["lora_w"]
    hf = b_f.shape[-1]
    tb = _pick(seq_rows, (128, 64, 32, 16, 8))
    lora_blk = (3 * prm["dr"] + 3 * prm["df"]) // dl
    p3 = p.reshape(n_seq, seq_rows, p.shape[-1])
    out = pl.BlockSpec((None, tb, hf), lambda s, i: (s, i, 0))
    return pl.pallas_call(
        functools.partial(_logf_kernel, tb=tb, pad=pad, f_off=prm["f_off"], n_heads=hf),
        grid=(n_seq, seq_rows // tb),
        in_specs=[pl.BlockSpec((None, tb, dl), lambda s, i: (s, i, lora_blk)),
                  pl.BlockSpec((1, hf), lambda s, i: (0, 0))],
        out_specs=[out, out],
        out_shape=[jax.ShapeDtypeStruct((n_seq, seq_rows, hf), F32)] * 2,
        scratch_shapes=[pltpu.VMEM((8, hf), F32)],
        compiler_params=_cparams(("parallel", "arbitrary")),
        name="logf_cumsum",
    )(p3, b_f.reshape(1, hf))


LOG2E = 1.4426950408889634


def _bf16_terms(x):
    hi = x.astype(BF16)
    r = x - hi.astype(F32)
    mid = r.astype(BF16)
    lo = (r - mid.astype(F32)).astype(BF16)
    return hi.astype(F32), mid.astype(F32), lo.astype(F32)


def _augment(x, own, lane, first, ones_at, terms):
    t0, t1, t2 = terms
    a = first
    extra = jnp.where(lane == a + (3 - ones_at), t0, jnp.where(lane == a + (4 - ones_at), t1,
            jnp.where(lane == a + (5 - ones_at), t2, 0.0)))
    extra = jnp.where((lane >= a + ones_at) & (lane < a + ones_at + 3), 1.0, extra)
    return jnp.where(own, x, extra).astype(BF16)


def _fox_prompt_kernel(q_ref, k_ref, v_ref, cq_ref, ck_ref, o_ref, ka_scr, va_scr, m_scr, acc_scr, *, tq, scale2):
    i = pl.program_id(2)
    seq_rows = k_ref.shape[0]

    @pl.when(i == 0)
    def _():
        lane_k = lax.broadcasted_iota(jnp.int32, (seq_rows, LANES), 1)
        k = k_ref[...]
        v = v_ref[...]
        for h in range(2):
            own = (lane_k < HEAD_DIM) if h == 0 else (lane_k >= HEAD_DIM)
            first = HEAD_DIM * (1 - h)
            ka_scr[h] = _augment(k, own, lane_k, first, 0, _bf16_terms(-ck_ref[:, h:h + 1]))
            va_scr[h] = jnp.where(own, v, 1.0).astype(BF16)

    lane = lax.broadcasted_iota(jnp.int32, (tq, LANES), 1)
    q = q_ref[...] * scale2
    qa = []
    for h in range(2):
        own = (lane < HEAD_DIM) if h == 0 else (lane >= HEAD_DIM)
        qa.append(_augment(q, own, lane, HEAD_DIM * (1 - h), 3, _bf16_terms(cq_ref[:, h:h + 1])))
    m_scr[...] = jnp.full_like(m_scr, NEG_INF)
    acc_scr[...] = jnp.zeros_like(acc_scr)

    def block(j, width, diagonal):
        off = pl.multiple_of(j * tq, tq)
        for h in range(2):
            s = _dot_nt(qa[h], ka_scr[h, pl.ds(off, width), :])
            if diagonal:
                rr = lax.broadcasted_iota(jnp.int32, (tq, width), 0)
                cc = lax.broadcasted_iota(jnp.int32, (tq, width), 1)
                s = jnp.where(cc <= rr, s, NEG_INF)
            cols = [s[:, c * LANES:(c + 1) * LANES] for c in range(width // LANES)]
            m_blk = jnp.max(functools.reduce(jnp.maximum, cols), axis=-1, keepdims=True)
            m_prev = m_scr[h]
            m_new = jnp.maximum(m_prev, m_blk)
            p = jnp.concatenate([jnp.exp2(c - m_new) for c in cols], axis=1).astype(BF16)
            acc_scr[h] = jnp.exp2(m_prev - m_new) * acc_scr[h] + _dot(p, va_scr[h, pl.ds(off, width), :])
            m_scr[h] = m_new

    def body(jj, carry):
        block(2 * jj, 2 * tq, False)
        return carry

    lax.fori_loop(0, i // 2, body, 0)

    @pl.when(i % 2 == 1)
    def _():
        block(i - 1, tq, False)

    block(i, tq, True)
    a0 = acc_scr[0]
    a1 = acc_scr[1]
    o = jnp.where(lane < HEAD_DIM, a0 / pltpu.roll(a0, HEAD_DIM, 1), a1 / pltpu.roll(a1, HEAD_DIM, 1))
    o_ref[...] = o.astype(BF16)


def _fox_prompt(p, c, prm, n_seq, seq_rows, pad):
    dr, df = prm["dr"], prm["df"]
    hf = df // HEAD_DIM
    n_pairs = hf // 2
    tq = _pick(seq_rows, (384, 256, 128))
    nq = seq_rows // tq
    p3 = p.reshape(n_seq, seq_rows, p.shape[-1])
    c4 = (c * LOG2E).reshape(n_seq, seq_rows, n_pairs, 2)
    cq = c4.transpose(0, 2, 1, 3)
    is_pad = (jnp.arange(seq_rows) < pad)[None, :, None, None]
    ck = jnp.where(is_pad, -NEG_INF, c4).transpose(0, 2, 1, 3)
    q0, k0, v0 = (3 * dr) // LANES, (3 * dr + df) // LANES, (3 * dr + 2 * df) // LANES
    return pl.pallas_call(
        functools.partial(_fox_prompt_kernel, tq=tq, scale2=HEAD_DIM ** -0.5 * LOG2E),
        grid=(n_seq, n_pairs, nq),
        in_specs=[pl.BlockSpec((None, tq, LANES), lambda b, pr, i: (b, i, q0 + pr)),
                  pl.BlockSpec((None, seq_rows, LANES), lambda b, pr, i: (b, 0, k0 + pr)),
                  pl.BlockSpec((None, seq_rows, LANES), lambda b, pr, i: (b, 0, v0 + pr)),
                  pl.BlockSpec((None, None, tq, 2), lambda b, pr, i: (b, pr, i, 0)),
                  pl.BlockSpec((None, None, seq_rows, 2), lambda b, pr, i: (b, pr, 0, 0))],
        out_specs=pl.BlockSpec((None, tq, LANES), lambda b, pr, i: (b, i, pr)),
        out_shape=jax.ShapeDtypeStruct((n_seq, seq_rows, df), BF16),
        scratch_shapes=[pltpu.VMEM((2, seq_rows, LANES), BF16), pltpu.VMEM((2, seq_rows, LANES), BF16),
                        pltpu.VMEM((2, tq, LANES), F32), pltpu.VMEM((2, tq, LANES), F32)],
        compiler_params=_cparams(("parallel", "parallel", "arbitrary")),
        name="fox_prompt",
    )(p3, p3, p3, cq, ck).reshape(n_seq * seq_rows, df)


def _fox_sample_kernel(pt_ref, q_ref, kn_ref, vn_ref, cn_ref, cnrow_ref, *rest, pp, page, nq, hf, scale):
    lf_refs = rest[:pp]
    k_refs = rest[pp:2 * pp]
    v_refs = rest[2 * pp:3 * pp]
    o_ref, qbd_scr, m_scr, l_scr, acc_scr, carry_scr = rest[3 * pp:]
    step = pl.program_id(1)
    gh = MXU_DEPTH // HEAD_DIM
    ng = hf // gh
    gr = gh * nq
    gd = gh * HEAD_DIM
    rows = hf * nq
    own = (_idiv(lax.broadcasted_iota(jnp.int32, (gr, gd), 0), nq)
           == _idiv(lax.broadcasted_iota(jnp.int32, (gr, gd), 1), HEAD_DIM))

    @pl.when(step == 0)
    def _():
        q = q_ref[...] * scale
        for g in range(ng):
            qg = jnp.broadcast_to(q[None, :, g * gd:(g + 1) * gd], (gh, nq, gd)).reshape(gr, gd)
            qbd_scr[g] = jnp.where(own, qg, 0.0).astype(BF16)
        m_scr[...] = jnp.full_like(m_scr, NEG_INF)
        l_scr[...] = jnp.zeros_like(l_scr)
        acc_scr[...] = jnp.zeros_like(acc_scr)
        carry_scr[...] = jnp.zeros_like(carry_scr)

    tr = lax.broadcasted_iota(jnp.int32, (page, 2 * page), 0)
    tc = lax.broadcasted_iota(jnp.int32, (page, 2 * page), 1)
    suffix = jnp.where((tc >= page) | (tr > tc), 1.0, 0.0).astype(BF16)
    carry = carry_scr[...]
    later_sum = [None] * pp
    for x in reversed(range(pp)):
        r = _dot_x01(lf_refs[x][...], suffix)
        later_sum[x] = r[:, :page] + carry
        carry = carry + r[:, page:]
    carry_scr[...] = carry
    bias = jnp.concatenate(later_sum, axis=1)
    bias = jnp.broadcast_to(bias[:, None, :], (hf, nq, pp * page)).reshape(rows, pp * page)

    group = lambda ref, g: ref[g * gd:(g + 1) * gd, :].astype(BF16)
    s = jnp.concatenate(
        [jnp.concatenate([_dot(qbd_scr[g], group(k_refs[x], g)) for g in range(ng)], axis=0) for x in range(pp)],
        axis=1)
    s = s + cn_ref[...] + bias
    m_prev = m_scr[...]
    m_new = jnp.maximum(m_prev, jnp.max(s, axis=-1, keepdims=True))
    alpha = jnp.exp(m_prev - m_new)
    p = jnp.exp(s - m_new)
    l_scr[...] = alpha * l_scr[...] + jnp.sum(p, axis=-1, keepdims=True)
    m_scr[...] = m_new
    for g in range(ng):
        rws = slice(g * gr, (g + 1) * gr)
        acc = alpha[rws] * acc_scr[g]
        for x in range(pp):
            acc = acc + _dot_nt(p[rws, x * page:(x + 1) * page].astype(BF16), group(v_refs[x], g))
        acc_scr[g] = acc

    @pl.when(step == pl.num_programs(1) - 1)
    def _():
        q = (q_ref[...] * scale).astype(BF16)
        kn = kn_ref[...].astype(BF16)
        vn = vn_ref[...].astype(BF16)
        head = lambda a, h: a[:, h * HEAD_DIM:(h + 1) * HEAD_DIM]
        sn = jnp.stack([_dot_nt(head(q, h), head(kn, h)) for h in range(hf)])
        qi = lax.broadcasted_iota(jnp.int32, (hf, nq, nq), 1)
        ki = lax.broadcasted_iota(jnp.int32, (hf, nq, nq), 2)
        sn = jnp.where(ki <= qi, sn + cnrow_ref[...], NEG_INF).reshape(rows, nq)
        m_last = m_scr[...]
        m_fin = jnp.maximum(m_last, jnp.max(sn, axis=-1, keepdims=True))
        a_fin = jnp.exp(m_last - m_fin)
        pn = jnp.exp(sn - m_fin)
        inv = 1.0 / (a_fin * l_scr[...] + jnp.sum(pn, axis=-1, keepdims=True))
        pn = (pn * inv).astype(BF16)
        o_new = jnp.concatenate([_dot(pn[h * nq:(h + 1) * nq, :], head(vn, h)) for h in range(hf)], axis=1)
        w_past = a_fin * inv
        o_past = jnp.concatenate(
            [jnp.sum(jnp.where(own, acc_scr[g] * w_past[g * gr:(g + 1) * gr], 0.0).reshape(gh, nq, gd), axis=0)
             for g in range(ng)], axis=1)
        o_ref[...] = (o_past + o_new).astype(BF16)


def _fox_sample(p_s, c_new, cache_k, cache_v, cache_lft, page_table_flat, prm, n_seq, nq, n_pages, layer):
    dr, df = prm["dr"], prm["df"]
    hf = df // HEAD_DIM
    page = cache_k.shape[3]
    gh = MXU_DEPTH // HEAD_DIM
    assert hf % gh == 0
    pp = _pick(n_pages, (4, 2, 1))
    npb = n_pages // pp
    rows = hf * nq
    p3 = p_s.reshape(n_seq, nq, p_s.shape[-1])
    cn_t = c_new.transpose(0, 2, 1)
    cn = cn_t.reshape(n_seq, rows, 1)
    cn_row = cn_t[..., None] - cn_t[:, :, None, :]
    qb, kb, vb = (3 * dr) // df, (3 * dr + df) // df, (3 * dr + 2 * df) // df
    seq_blk = lambda blk: pl.BlockSpec((None, nq, df), lambda b, i, pt: (b, 0, blk))

    def page_of(b, i, pt, x):
        return pt[b * n_pages + (npb - 1 - i) * pp + x]

    def kv_spec(x):
        return pl.BlockSpec((None, None, df, page), lambda b, i, pt: (layer, page_of(b, i, pt, x), 0, 0))

    def lf_spec(x):
        return pl.BlockSpec((None, None, hf, page), lambda b, i, pt: (layer, page_of(b, i, pt, x), 0, 0))

    grid_spec = pltpu.PrefetchScalarGridSpec(
        num_scalar_prefetch=1,
        grid=(n_seq, npb),
        in_specs=[seq_blk(qb), seq_blk(kb), seq_blk(vb),
                  pl.BlockSpec((None, rows, 1), lambda b, i, pt: (b, 0, 0)),
                  pl.BlockSpec((None, hf, nq, nq), lambda b, i, pt: (b, 0, 0, 0))]
                 + [lf_spec(x) for x in range(pp)] + [kv_spec(x) for x in range(pp)] * 2,
        out_specs=pl.BlockSpec((None, nq, df), lambda b, i, pt: (b, 0, 0)),
        scratch_shapes=[pltpu.VMEM((hf // gh, gh * nq, MXU_DEPTH), BF16), pltpu.VMEM((rows, 1), F32),
                        pltpu.VMEM((rows, 1), F32), pltpu.VMEM((hf // gh, gh * nq, MXU_DEPTH), F32),
                        pltpu.VMEM((hf, page), F32)],
    )
    out = pl.pallas_call(
        functools.partial(_fox_sample_kernel, pp=pp, page=page, nq=nq, hf=hf, scale=HEAD_DIM ** -0.5),
        grid_spec=grid_spec,
        out_shape=jax.ShapeDtypeStruct((n_seq, nq, df), BF16),
        compiler_params=_cparams(("parallel", "arbitrary")),
        name="fox_sample",
    )(page_table_flat, p3, p3, p3, cn, cn_row, *([cache_lft] * pp), *([cache_k] * pp), *([cache_v] * pp))
    return out.reshape(n_seq * nq, df)


def _gelu_tanh(x):
    return 0.5 * x * (1.0 + jnp.tanh(0.7978845608028654 * (x + 0.044715 * (x * x * x))))


def _ffn_in_kernel(*refs, tm, tn, seq_rows, pad, sample):
    if sample:
        x_ref, wu_ref, wg_ref, cw_ref, cb_ref, p1_ref, p2_ref, h_ref, u_ref = refs
    else:
        x_ref, wu_ref, wg_ref, cw_ref, cb_ref, h_ref, tail_ref, carry = refs
    i = pl.program_id(0)
    j = pl.program_id(1)
    x = x_ref[...]
    u = _dot(x, wu_ref[...])
    gate = _dot(x, wg_ref[...])
    row = lax.broadcasted_iota(jnp.int32, (tm, 1), 0)
    if sample:
        t = _imod(row, seq_rows)
        u1 = jnp.where(t == 0, p1_ref[...], pltpu.roll(u, 1, 0))
        u2 = jnp.where(t < 2, p2_ref[...], pltpu.roll(u, 2, 0))
        u_ref[...] = u
    else:
        @pl.when(i == 0)
        def _():
            carry[:, pl.ds(pl.multiple_of(j * tn, tn), tn)] = jnp.zeros((8, tn), F32)

        pos = (i * tm) % seq_rows + row
        u = jnp.where(pos >= pad, u, 0.0)
        prev = carry[:, pl.ds(pl.multiple_of(j * tn, tn), tn)]
        u1 = jnp.where(row == 0, prev[7:8, :], pltpu.roll(u, 1, 0))
        u2 = jnp.where(row == 0, prev[6:7, :], jnp.where(row == 1, prev[7:8, :], pltpu.roll(u, 2, 0)))
        tail = u[tm - 8:tm, :]
        carry[:, pl.ds(pl.multiple_of(j * tn, tn), tn)] = tail
        tail_ref[...] = tail
    c = cb_ref[...] + cw_ref[0:1, :] * u2 + cw_ref[1:2, :] * u1 + cw_ref[2:3, :] * u
    h_ref[...] = (_gelu_tanh(c) * gate).astype(BF16)


def _ffn_in(x1b, w_ffn_in_b, conv_w, conv_b, p1, p2, *, seq_rows, pad, sample):
    m, d = x1b.shape
    dff = conv_w.shape[-1]
    tn = _pick(dff, (256, 128))
    nj = dff // tn
    tm = m if sample else _pick(seq_rows, (528, 512, 384, 256, 128, 64, 32, 16, 8))
    ni = m // tm
    in_specs = [pl.BlockSpec((tm, d), lambda i, j: (i, 0)),
                pl.BlockSpec((d, tn), lambda i, j: (0, j)),
                pl.BlockSpec((d, tn), lambda i, j: (0, j + nj)),
                pl.BlockSpec((3, tn), lambda i, j: (0, j)),
                pl.BlockSpec((1, tn), lambda i, j: (0, j))]
    args = [x1b, w_ffn_in_b, w_ffn_in_b, conv_w, conv_b.reshape(1, dff)]
    tile = pl.BlockSpec((tm, tn), lambda i, j: (i, j))
    if sample:
        in_specs += [tile, tile]
        args += [p1, p2]
        out_specs = [tile, tile]
        out_shape = [jax.ShapeDtypeStruct((m, dff), BF16), jax.ShapeDtypeStruct((m, dff), F32)]
        scratch = []
    else:
        out_specs = [tile, pl.BlockSpec((8, tn), lambda i, j: (i, j))]
        out_shape = [jax.ShapeDtypeStruct((m, dff), BF16), jax.ShapeDtypeStruct((ni * 8, dff), F32)]
        scratch = [pltpu.VMEM((8, dff), F32)]
    return pl.pallas_call(
        functools.partial(_ffn_in_kernel, tm=tm, tn=tn, seq_rows=seq_rows, pad=pad, sample=sample),
        grid=(ni, nj),
        in_specs=in_specs,
        out_specs=out_specs,
        out_shape=out_shape,
        scratch_shapes=scratch,
        compiler_params=_cparams(("arbitrary", "arbitrary")),
        name="ffn_in_sample" if sample else "ffn_in_prompt",
    )(*args)


def _layer_params(lw, d_model):
    (w_in, b_f, mu_shift, w0, w_decay_up, a0, w_iclr_up, w_gate_up, k_k, k_a, r_k,
     gn_g, gn_b, w_out, ln1_g, ln1_b, w_ffn_in, conv_w, conv_b, w_ffn_out, ln2_g, ln2_b) = lw
    dr = w0.shape[-1]
    hf = b_f.shape[-1]
    df = hf * HEAD_DIM
    d_dec, d_icl, d_gate = w_decay_up.shape[0], w_iclr_up.shape[0], w_gate_up.shape[0]
    rw_cols = 3 * dr + d_dec + d_icl + d_gate
    n_lora = d_dec + d_icl + d_gate
    lora_w = -(-(n_lora + hf) // (4 * LANES)) * (4 * LANES)
    assert (3 * dr + 3 * df) % lora_w == 0 and dr % LANES == 0 and df % LANES == 0
    pk, pv = _rwkv_col_orders(dr // HEAD_DIM)
    on_k = lambda a: a[..., pk]
    on_v = lambda a: a[..., pv]
    fox0 = rw_cols
    w_perm = jnp.concatenate(
        [on_k(w_in[:, :dr]), on_k(w_in[:, dr:2 * dr]), on_v(w_in[:, 2 * dr:3 * dr]),
         w_in[:, fox0:fox0 + 3 * df], w_in[:, 3 * dr:rw_cols],
         w_in[:, fox0 + 3 * df:], jnp.zeros((d_model, lora_w - n_lora - hf), w_in.dtype)], axis=1).astype(BF16)

    def pad_rows(w, off):
        return jnp.zeros((lora_w, dr), F32).at[off:off + w.shape[0]].set(w).astype(BF16)

    w_out_perm = jnp.concatenate([w_out[:dr][pv], w_out[dr:]], axis=0).astype(BF16)
    prm = dict(
        dr=dr, df=df, lora_w=lora_w, d_dec=d_dec, d_icl=d_icl, f_off=n_lora, rw_cols=rw_cols, pk=pk, pv=pv,
        mu_rkv=jnp.concatenate([on_k(mu_shift[:dr]), on_k(mu_shift[dr:2 * dr]),
                                on_v(mu_shift[2 * dr:3 * dr])]).reshape(1, -1),
        mu_lora=jnp.zeros((1, lora_w), F32).at[0, :n_lora].set(mu_shift[3 * dr:]),
        wd=pad_rows(on_k(w_decay_up), 0), wa=pad_rows(on_k(w_iclr_up), d_dec),
        wg=pad_rows(on_v(w_gate_up), d_dec + d_icl),
        w0=on_k(w0).reshape(1, dr), a0=on_k(a0).reshape(1, dr), k_k=on_k(k_k).reshape(1, dr),
        k_a=on_k(k_a).reshape(1, dr), r_k=on_k(r_k.reshape(dr)).reshape(1, dr), w_perm=w_perm,
        w_out=w_out_perm, w_ffn_in=w_ffn_in.astype(BF16), w_ffn_out=w_ffn_out.astype(BF16), b_f=b_f,
        gn_g=on_v(gn_g), gn_b=on_v(gn_b), ln1=(ln1_g, ln1_b), ln2=(ln2_g, ln2_b), conv_w=conv_w, conv_b=conv_b)
    return prm


def _mm_tiles(m, n, seq_rows):
    tm = _pick(seq_rows, (1056, 1024, 768, 512, 384, 256, 128, 64, 32, 16, 8)) if m > 512 else m
    tn = _pick(n, (640, 512, 256, 128))
    return tm, tn


def _permute_rw_row(row, prm):
    dr = prm["dr"]
    n_lora = prm["f_off"]
    lora = jnp.zeros(row.shape[:-1] + (prm["lora_w"],), F32).at[..., :n_lora].set(row[..., 3 * dr:])
    rkv = jnp.concatenate([row[..., :dr][..., prm["pk"]], row[..., dr:2 * dr][..., prm["pk"]],
                           row[..., 2 * dr:3 * dr][..., prm["pv"]]], axis=-1)
    return rkv, lora


def _unpermute_rkv(rkv, prm):
    dr = prm["dr"]
    ik, iv = np.argsort(prm["pk"]), np.argsort(prm["pv"])
    return jnp.concatenate([rkv[..., :dr][..., ik], rkv[..., dr:2 * dr][..., ik],
                            rkv[..., 2 * dr:3 * dr][..., iv]], axis=-1)


def _run_group(x2, ln0, prm, alpha, *, n_seq, seq_rows, pad, skip, sample, shift_prev, s0, ffn_prev, attend):
    ln0_g, ln0_b = ln0
    m, d = x2.shape
    dr, df = prm["dr"], prm["df"]
    h_b = _ln0(x2, ln0_g, ln0_b)
    tm, tn = _mm_tiles(m, prm["w_perm"].shape[1], seq_rows)
    p = _matmul(h_b, prm["w_perm"], tm, tn, "in_proj")

    p3 = p.reshape(n_seq, seq_rows, -1)
    if sample:
        sp_rkv, sp_lora = _permute_rw_row(shift_prev, prm)
        expand = lambda a: jnp.zeros((n_seq, seq_rows, a.shape[-1]), F32).at[:, 0].set(a)
        sp_rkv, sp_lora = expand(sp_rkv), expand(sp_lora)
    else:
        sp_rkv = sp_lora = None
    r, w, k, v, kk, b, g, bonus = _rwkv_prep(p3, sp_rkv, sp_lora, prm, pad=pad, sample=sample)
    o_scan, s_last = _rwkv_scan(r, w, k, kk, b, v, s0)
    o_rw = _rwkv_post(o_scan, g, bonus, prm["gn_g"], prm["gn_b"]).reshape(m, dr)

    lf, c = _logf_cumsum(p, prm["b_f"], prm, n_seq, seq_rows, pad)
    o_fox = attend(p, c)

    mix_in = jnp.concatenate([o_rw, o_fox], axis=1)
    tm, tn = _mm_tiles(m, d, seq_rows)
    mix = _matmul(mix_in, prm["w_out"], tm, tn, "out_proj")
    x1, x1b = _ln1(x2, mix, ln0_g, ln0_b, *prm["ln1"], alpha)

    if sample:
        p1 = jnp.zeros((n_seq, seq_rows, ffn_prev.shape[-1]), F32).at[:, 0].set(ffn_prev[:, 1])
        p2 = p1.at[:, 0].set(ffn_prev[:, 0]).at[:, 1].set(ffn_prev[:, 1])
        p1, p2 = p1.reshape(m, -1), p2.reshape(m, -1)
    else:
        p1 = p2 = None
    hid, u_aux = _ffn_in(x1b, prm["w_ffn_in"], prm["conv_w"], prm["conv_b"], p1, p2,
                         seq_rows=seq_rows, pad=pad, sample=sample)
    dff = hid.shape[1]
    tm_o = _pick(seq_rows, (528, 512, 384, 256, 128, 64, 32, 16, 8)) if m > 512 else m
    ffn = _matmul(hid, prm["w_ffn_out"], tm_o, _pick(d, (256, 128)), "ffn_out")
    y = _ln2(x1, ffn, *prm["ln2"], alpha, n_seq, seq_rows, skip)

    hf = df // HEAD_DIM
    k_new = p3[:, pad:, 3 * dr + df:3 * dr + 2 * df].reshape(n_seq, seq_rows - pad, hf, HEAD_DIM)
    v_new = p3[:, pad:, 3 * dr + 2 * df:3 * dr + 3 * df].reshape(n_seq, seq_rows - pad, hf, HEAD_DIM)
    last = p3[:, -1]
    shift_last = jnp.concatenate([_unpermute_rkv(last[:, :3 * dr], prm),
                                  last[:, 3 * dr + 3 * df:3 * dr + 3 * df + prm["f_off"]]], axis=1)
    if sample:
        conv_state = u_aux.reshape(n_seq, seq_rows, dff)[:, -2:]
    else:
        tiles_per_seq = u_aux.shape[0] // 8 // n_seq
        conv_state = u_aux.reshape(n_seq, tiles_per_seq, 8, dff)[:, -1, -2:]
    return y, (k_new, v_new, lf[:, pad:], s_last, shift_last, conv_state)


def kernel(x_prompt, x_sample, cache_k, cache_v, cache_logf, state_rwkv, state_shift, state_ffn_conv,
           page_table, meta_tokens, ln0_g, ln0_b, w_in, b_f, mu_shift, w0, w_decay_up, a0, w_iclr_up,
           w_gate_up, k_k, k_a, r_k, gn_g, gn_b, w_out, ln1_g, ln1_b, w_ffn_in, conv_w, conv_b,
           w_ffn_out, ln2_g, ln2_b):
    depth = w_in.shape[0]
    assert depth == 1, "the token layout between layers is only wired for a single layer"
    b, seq, d = x_prompt.shape
    db, nq, _ = x_sample.shape
    n_meta = meta_tokens.shape[0]
    alpha = (2 * depth) ** 0.25
    n_pool, page = cache_k.shape[1:3]
    n_pages = page_table.shape[1]
    h_rw = r_k.shape[1]

    t_real = seq + n_meta
    pad = (-t_real) % Q_BLOCK
    t_pad = t_real + pad
    meta = jnp.broadcast_to(meta_tokens[None].astype(x_prompt.dtype), (b, n_meta, d))
    xp = jnp.concatenate([jnp.zeros((b, pad, d), x_prompt.dtype), meta, x_prompt], axis=1).reshape(b * t_pad, d)
    xs = x_sample.reshape(db * nq, d)
    pt_flat = page_table.reshape(-1).astype(jnp.int32)

    outs_p, outs_s = [], []
    for l in range(depth):
        lw = tuple(wt[l] for wt in (w_in, b_f, mu_shift, w0, w_decay_up, a0, w_iclr_up, w_gate_up, k_k, k_a,
                                     r_k, gn_g, gn_b, w_out, ln1_g, ln1_b, w_ffn_in, conv_w, conv_b,
                                     w_ffn_out, ln2_g, ln2_b))
        prm = _layer_params(lw, d)
        hf = prm["df"] // HEAD_DIM

        attend_p = lambda p, c: _fox_prompt(p, c, prm, b, t_pad, pad)
        y_p, st_p = _run_group(xp, (ln0_g, ln0_b), prm, alpha, n_seq=b, seq_rows=t_pad, pad=pad, skip=pad + n_meta, sample=False,
                               shift_prev=None, s0=jnp.zeros((b, h_rw, HEAD_DIM, HEAD_DIM), F32),
                               ffn_prev=None, attend=attend_p)

        cache_kt = cache_k.transpose(0, 1, 3, 4, 2).reshape(depth, n_pool, hf * HEAD_DIM, page)
        cache_vt = cache_v.transpose(0, 1, 3, 4, 2).reshape(depth, n_pool, hf * HEAD_DIM, page)
        cache_lft = cache_logf.astype(F32).transpose(0, 1, 3, 2)
        attend_s = lambda p, c: _fox_sample(p, c, cache_kt, cache_vt, cache_lft, pt_flat, prm, db, nq, n_pages, l)
        y_s, st_s = _run_group(xs, (ln0_g, ln0_b), prm, alpha, n_seq=db, seq_rows=nq, pad=0, skip=0, sample=True,
                               shift_prev=state_shift[l], s0=state_rwkv[l], ffn_prev=state_ffn_conv[l],
                               attend=attend_s)
        outs_p.append(st_p)
        outs_s.append(st_s)

    k_p, v_p, lf_p, rw_p, sh_p, cv_p = (jnp.stack([o[i] for o in outs_p]) for i in range(6))
    k_s, v_s, lf_s, rw_s, sh_s, cv_s = (jnp.stack([o[i] for o in outs_s]) for i in range(6))
    y_prompt = y_p
    y_sample = y_s.reshape(db, nq, d)
    return (y_prompt, y_sample, k_p, v_p, lf_p, rw_p, sh_p, cv_p, k_s, v_s, lf_s, rw_s, sh_s, cv_s)
```

```python
import functools

import jax
import jax.numpy as jnp
import numpy as np
from jax import lax
from jax.experimental import pallas as pl
from jax.experimental.pallas import tpu as pltpu

F32 = jnp.float32
BF16 = jnp.bfloat16

HEAD_DIM = 64
LANES = 128
SUBLANES = 8
MXU_DEPTH = 256
Q_BLOCK = 128
LN_EPS = 1e-5
GN_EPS = 64e-5
NEG_INF = -1e30
VMEM_LIMIT = 56 * 1024 * 1024


def _cparams(sem):
    return pltpu.CompilerParams(dimension_semantics=sem, vmem_limit_bytes=VMEM_LIMIT)


def _pick(n, candidates):
    for c in candidates:
        if n % c == 0:
            return c
    return n


def _split_bf16(x):
    hi = x.astype(BF16)
    lo = (x - hi.astype(F32)).astype(BF16)
    return hi, lo


def _dot(a, b):
    return jnp.dot(a, b, preferred_element_type=F32)


def _dot_nt(a, b):
    return lax.dot_general(a, b, (((1,), (1,)), ((), ())), preferred_element_type=F32)


def _dot_x01(x, m01):
    hi, lo = _split_bf16(x)
    return _dot(hi, m01) + _dot(lo, m01)


def _dot_01x(m01, x):
    hi, lo = _split_bf16(x)
    return _dot(m01, hi) + _dot(m01, lo)


def _idiv(x, c):
    return lax.shift_right_logical(x, c.bit_length() - 1) if c & (c - 1) == 0 else x // c


def _imod(x, c):
    return (x & (c - 1)) if c & (c - 1) == 0 else x % c


def _head_select(n_cols, n_heads):
    c = lax.broadcasted_iota(jnp.int32, (n_cols, n_heads), 0)
    h = lax.broadcasted_iota(jnp.int32, (n_cols, n_heads), 1)
    return jnp.where(_idiv(c, HEAD_DIM) == h, 1.0, 0.0).astype(BF16)


def _head_expand(n_heads, n_cols):
    h = lax.broadcasted_iota(jnp.int32, (n_heads, n_cols), 0)
    c = lax.broadcasted_iota(jnp.int32, (n_heads, n_cols), 1)
    return jnp.where(_idiv(c, HEAD_DIM) == h, 1.0, 0.0).astype(BF16)


def _head_sum_bcast(x, sel, exp):
    return _dot_x01(_dot_x01(x, sel), exp)


def _softplus(z):
    return jnp.maximum(z, 0.0) + jnp.log1p(jnp.exp(-jnp.abs(z)))


def _sigmoid(z):
    return 1.0 / (1.0 + jnp.exp(-z))


def _ln_rows(x, g, b):
    mu = jnp.mean(x, axis=-1, keepdims=True)
    xc = x - mu
    var = jnp.mean(xc * xc, axis=-1, keepdims=True)
    return xc * lax.rsqrt(var + LN_EPS) * g + b


def _ln0_kernel(x_ref, g_ref, b_ref, o_ref):
    o_ref[...] = _ln_rows(x_ref[...], g_ref[...], b_ref[...]).astype(o_ref.dtype)


def _ln0(x2, g, b):
    m, d = x2.shape
    tm = _pick(m, (528, 512, 384, 256, 128, 64, 32, 16, 8))
    return pl.pallas_call(
        _ln0_kernel,
        grid=(m // tm,),
        in_specs=[pl.BlockSpec((tm, d), lambda i: (i, 0)),
                  pl.BlockSpec((1, d), lambda i: (0, 0)),
                  pl.BlockSpec((1, d), lambda i: (0, 0))],
        out_specs=pl.BlockSpec((tm, d), lambda i: (i, 0)),
        out_shape=jax.ShapeDtypeStruct((m, d), BF16),
        compiler_params=_cparams(("parallel",)),
        name="ln0",
    )(x2, g.reshape(1, d), b.reshape(1, d))


def _ln1_kernel(x_ref, mix_ref, g0_ref, b0_ref, g_ref, b_ref, of_ref, ob_ref, *, alpha):
    h = _ln_rows(x_ref[...], g0_ref[...], b0_ref[...])
    x1 = _ln_rows(alpha * h + mix_ref[...], g_ref[...], b_ref[...])
    of_ref[...] = x1
    ob_ref[...] = x1.astype(BF16)


def _ln1(x2, mix, g0, b0, g, b, alpha):
    m, d = x2.shape
    tm = _pick(m, (264, 256, 128, 64, 32, 16, 8))
    row = pl.BlockSpec((tm, d), lambda i: (i, 0))
    vec = pl.BlockSpec((1, d), lambda i: (0, 0))
    return pl.pallas_call(
        functools.partial(_ln1_kernel, alpha=alpha),
        grid=(m // tm,),
        in_specs=[row, row, vec, vec, vec, vec],
        out_specs=[row, row],
        out_shape=[jax.ShapeDtypeStruct((m, d), F32), jax.ShapeDtypeStruct((m, d), BF16)],
        compiler_params=_cparams(("parallel",)),
        name="ln1",
    )(x2, mix, g0.reshape(1, d), b0.reshape(1, d), g.reshape(1, d), b.reshape(1, d))


def _ln2_kernel(x1_ref, ffn_ref, g_ref, b_ref, o_ref, *, alpha):
    o_ref[...] = _ln_rows(alpha * x1_ref[...] + ffn_ref[...], g_ref[...], b_ref[...])


def _ln2(x1, ffn, g, b, alpha, n_seq, seq_rows, skip_rows):
    d = x1.shape[-1]
    out_rows = seq_rows - skip_rows
    tb = next(c for c in (256, 128, 64, 32, 16, 8) if skip_rows % c == 0 and out_rows % c == 0)
    skip = skip_rows // tb
    x3 = x1.reshape(n_seq, seq_rows, d)
    f3 = ffn.reshape(n_seq, seq_rows, d)
    row_in = pl.BlockSpec((None, tb, d), lambda s, i: (s, i + skip, 0))
    vec = pl.BlockSpec((1, d), lambda s, i: (0, 0))
    return pl.pallas_call(
        functools.partial(_ln2_kernel, alpha=alpha),
        grid=(n_seq, out_rows // tb),
        in_specs=[row_in, row_in, vec, vec],
        out_specs=pl.BlockSpec((None, tb, d), lambda s, i: (s, i, 0)),
        out_shape=jax.ShapeDtypeStruct((n_seq, out_rows, d), F32),
        compiler_params=_cparams(("parallel", "parallel")),
        name="ln2",
    )(x3, f3, g.reshape(1, d), b.reshape(1, d))


def _mm_kernel(x_ref, w_ref, o_ref):
    o_ref[...] = _dot(x_ref[...], w_ref[...])


def _matmul(x, w, tm, tn, name):
    m, k = x.shape
    n = w.shape[1]
    return pl.pallas_call(
        _mm_kernel,
        grid=(m // tm, n // tn),
        in_specs=[pl.BlockSpec((tm, k), lambda i, j: (i, 0)),
                  pl.BlockSpec((k, tn), lambda i, j: (0, j))],
        out_specs=pl.BlockSpec((tm, tn), lambda i, j: (i, j)),
        out_shape=jax.ShapeDtypeStruct((m, n), F32),
        compiler_params=_cparams(("parallel", "arbitrary")),
        name=name,
    )(x, w)


def _mm2_kernel(xa_ref, xb_ref, wa_ref, wb_ref, o_ref):
    o_ref[...] = _dot(xa_ref[...], wa_ref[...]) + _dot(xb_ref[...], wb_ref[...])


def _matmul2(xa, xb, w, tm, tn, name):
    m, ka = xa.shape
    kb = xb.shape[1]
    n = w.shape[1]
    assert ka == kb, "the weight's second row block is addressed as block index 1"
    return pl.pallas_call(
        _mm2_kernel,
        grid=(m // tm, n // tn),
        in_specs=[pl.BlockSpec((tm, ka), lambda i, j: (i, 0)),
                  pl.BlockSpec((tm, kb), lambda i, j: (i, 0)),
                  pl.BlockSpec((ka, tn), lambda i, j: (0, j)),
                  pl.BlockSpec((kb, tn), lambda i, j: (1, j))],
        out_specs=pl.BlockSpec((tm, tn), lambda i, j: (i, j)),
        out_shape=jax.ShapeDtypeStruct((m, n), F32),
        compiler_params=_cparams(("parallel", "arbitrary")),
        name=name,
    )(xa, xb, w, w)


CHAINS = LANES // 2
KH = HEAD_DIM // 2
KQ = KH // 2
VQ = HEAD_DIM // 4


def _rwkv_col_orders(n_heads):
    assert 4 * n_heads == LANES
    c = np.arange(n_heads * HEAD_DIM)
    h, rest = c % n_heads, c // n_heads
    pk = h * HEAD_DIM + ((rest // 2) % 2) * KH + (rest % 2) * KQ + rest // 4
    pv = h * HEAD_DIM + (rest % 4) * VQ + rest // 4
    return pk, pv


def _head_select_minor(n_cols, n_heads):
    c = lax.broadcasted_iota(jnp.int32, (n_cols, n_heads), 0)
    h = lax.broadcasted_iota(jnp.int32, (n_cols, n_heads), 1)
    return jnp.where(_imod(c, n_heads) == h, 1.0, 0.0).astype(BF16)


def _head_expand_minor(n_heads, n_cols):
    h = lax.broadcasted_iota(jnp.int32, (n_heads, n_cols), 0)
    c = lax.broadcasted_iota(jnp.int32, (n_heads, n_cols), 1)
    return jnp.where(_imod(c, n_heads) == h, 1.0, 0.0).astype(BF16)


def _pair_to_chain_k(x0, x1):
    tm = x0.shape[0]
    a = x0.reshape(tm, KQ, LANES)
    b = x1.reshape(tm, KQ, LANES)
    lane = lax.broadcasted_iota(jnp.int32, (tm, KQ, LANES), 2)
    seq0 = _imod(lane, CHAINS) < CHAINS // 2
    t0 = jnp.where(seq0, a, pltpu.roll(b, CHAINS // 2, 2))
    t1 = jnp.where(seq0, pltpu.roll(a, LANES - CHAINS // 2, 2), b)
    return jnp.concatenate([t0, t1], axis=1)


def _pair_to_chain_v(x0, x1):
    tm = x0.shape[0]
    a = x0.reshape(tm, VQ, LANES)
    b = x1.reshape(tm, VQ, LANES)
    q = CHAINS // 2
    grp = _idiv(lax.broadcasted_iota(jnp.int32, (tm, VQ, LANES), 2), q)
    outs = []
    for j in range(4):
        pa = jnp.where(grp == j, a, 0.0)
        pb = jnp.where(grp == j, b, 0.0)
        sa, sb = (LANES - q * j) % LANES, (q - q * j) % LANES
        y = (pltpu.roll(pa, sa, 2) if sa else pa) + (pltpu.roll(pb, sb, 2) if sb else pb)
        outs.append(y + pltpu.roll(y, CHAINS, 2))
    return jnp.concatenate(outs, axis=1)


def _chain_to_pair_v(o):
    tm = o.shape[0]
    q = CHAINS // 2
    grp = _idiv(lax.broadcasted_iota(jnp.int32, (tm, VQ, LANES), 2), q)
    res = []
    for s in range(2):
        acc = jnp.zeros((tm, VQ, LANES), F32)
        for j in range(4):
            piece = o[:, j * VQ:(j + 1) * VQ, :]
            sh = (q * j - q * s) % LANES
            acc = jnp.where(grp == j, pltpu.roll(piece, sh, 2) if sh else piece, acc)
        res.append(acc.reshape(tm, VQ * LANES))
    return res


def _rwkv_prep_kernel(*refs, pad, tm, dr, n_heads, sample):
    if sample:
        (rkv_ref, lora_ref, sp_rkv_ref, sp_lora_ref, mu_rkv_ref, mu_lora_ref, wd_ref, wa_ref, wg_ref,
         w0_ref, a0_ref, kk_s_ref, ka_s_ref, rk_ref,
         r_o, w_o, k_o, v_o, kk_o, b_o, g_o, bon_o) = refs
    else:
        (rkv_ref, lora_ref, mu_rkv_ref, mu_lora_ref, wd_ref, wa_ref, wg_ref,
         w0_ref, a0_ref, kk_s_ref, ka_s_ref, rk_ref,
         r_o, w_o, k_o, v_o, kk_o, b_o, g_o, bon_o, carry_rkv, carry_lora) = refs
    i = pl.program_id(1)
    row = lax.broadcasted_iota(jnp.int32, (tm, 1), 0)
    sel = _head_select_minor(dr, n_heads)
    exp = _head_expand_minor(n_heads, dr)

    if not sample:
        @pl.when(i == 0)
        def _():
            carry_rkv[...] = jnp.zeros_like(carry_rkv)
            carry_lora[...] = jnp.zeros_like(carry_lora)

    per_seq = []
    for s in range(2):
        p_rkv = rkv_ref[s]
        p_lora = lora_ref[s]
        if sample:
            prev_rkv = jnp.where(row == 0, sp_rkv_ref[s], pltpu.roll(p_rkv, 1, 0))
            prev_lora = jnp.where(row == 0, sp_lora_ref[s], pltpu.roll(p_lora, 1, 0))
        else:
            pos = i * tm + row
            real = pos >= pad
            p_rkv = jnp.where(real, p_rkv, 0.0)
            p_lora = jnp.where(real, p_lora, 0.0)
            prev_rkv = jnp.where(row == 0, carry_rkv[s, 0:1, :], pltpu.roll(p_rkv, 1, 0))
            prev_lora = jnp.where(row == 0, carry_lora[s, 0:1, :], pltpu.roll(p_lora, 1, 0))
            prev_rkv = jnp.where(pos > pad, prev_rkv, 0.0)
            prev_lora = jnp.where(pos > pad, prev_lora, 0.0)
            carry_rkv[s, 0:1, :] = p_rkv[tm - 1:tm, :]
            carry_lora[s, 0:1, :] = p_lora[tm - 1:tm, :]

        xs_rkv = p_rkv + (prev_rkv - p_rkv) * mu_rkv_ref[...]
        xl = p_lora + (prev_lora - p_lora) * mu_lora_ref[...]
        r = xs_rkv[:, 0:dr]
        k = xs_rkv[:, dr:2 * dr]
        v = xs_rkv[:, 2 * dr:3 * dr]

        dec = _dot(jnp.tanh(xl).astype(BF16), wd_ref[...])
        a_pre = _dot(xl.astype(BF16), wa_ref[...])
        g = _dot(_sigmoid(xl).astype(BF16), wg_ref[...])
        w_log = -_softplus(-(w0_ref[...] + dec)) - 0.5
        decay = jnp.exp(-jnp.exp(w_log))
        a = _sigmoid(a0_ref[...] + a_pre)

        kk = k * kk_s_ref[...]
        nrm = jnp.sqrt(_head_sum_bcast(kk * kk, sel, exp))
        kk = kk / jnp.maximum(nrm, 1e-12)
        k_mod = k * (1.0 + (a - 1.0) * ka_s_ref[...])
        g_o[s] = g
        bon_o[s] = _head_sum_bcast(r * k_mod * rk_ref[...], sel, exp) * v
        per_seq.append((r, decay, k_mod, kk, kk * a, v))

    for idx, out in enumerate((r_o, w_o, k_o, kk_o, b_o)):
        out[...] = _pair_to_chain_k(per_seq[0][idx], per_seq[1][idx])
    v_o[...] = _pair_to_chain_v(per_seq[0][5], per_seq[1][5])


def _rwkv_prep(p3, sp_rkv, sp_lora, prm, *, pad, sample):
    n_seq, t, _ = p3.shape
    assert n_seq % 2 == 0
    n_pairs = n_seq // 2
    dr = prm["dr"]
    dl = prm["lora_w"]
    tm = t if sample else _pick(t, (64, 32, 16, 8))
    lora_blk = (3 * dr + 3 * prm["df"]) // dl
    full = lambda a: pl.BlockSpec(a.shape, lambda p, i: (0,) * a.ndim)
    consts = [prm["mu_rkv"], prm["mu_lora"], prm["wd"], prm["wa"], prm["wg"], prm["w0"], prm["a0"],
              prm["k_k"], prm["k_a"], prm["r_k"]]
    in_specs = [pl.BlockSpec((2, tm, 3 * dr), lambda p, i: (p, i, 0)),
                pl.BlockSpec((2, tm, dl), lambda p, i: (p, i, lora_blk))]
    args = [p3, p3]
    if sample:
        in_specs += [pl.BlockSpec((2, tm, 3 * dr), lambda p, i: (p, i, 0)),
                     pl.BlockSpec((2, tm, dl), lambda p, i: (p, i, 0))]
        args += [sp_rkv, sp_lora]
    in_specs += [full(c) for c in consts]
    args += consts
    kspec = pl.BlockSpec((tm, KH, LANES), lambda p, i: (i, 0, p))
    vspec = pl.BlockSpec((tm, HEAD_DIM, LANES), lambda p, i: (i, 0, p))
    rspec = pl.BlockSpec((2, tm, dr), lambda p, i: (p, i, 0))
    kshape = jax.ShapeDtypeStruct((t, KH, n_pairs * LANES), F32)
    vshape = jax.ShapeDtypeStruct((t, HEAD_DIM, n_pairs * LANES), F32)
    rshape = jax.ShapeDtypeStruct((n_seq, t, dr), F32)
    scratch = [] if sample else [pltpu.VMEM((2, 8, 3 * dr), F32), pltpu.VMEM((2, 8, dl), F32)]
    return pl.pallas_call(
        functools.partial(_rwkv_prep_kernel, pad=pad, tm=tm, dr=dr, n_heads=dr // HEAD_DIM, sample=sample),
        grid=(n_pairs, t // tm),
        in_specs=in_specs,
        out_specs=[kspec, kspec, kspec, vspec, kspec, kspec, rspec, rspec],
        out_shape=[kshape, kshape, kshape, vshape, kshape, kshape, rshape, rshape],
        scratch_shapes=scratch,
        compiler_params=_cparams(("parallel", "arbitrary")),
        name="rwkv_prep_sample" if sample else "rwkv_prep_prompt",
    )(*args)


def _rwkv_scan_kernel(r_ref, w_ref, k_ref, kk_ref, b_ref, v_ref, s0_ref, o_ref, sf_ref, s_scr, *, tb):
    t_blk = pl.program_id(1)

    @pl.when(t_blk == 0)
    def _():
        s_scr[...] = s0_ref[...]

    def step(t, carry):
        r = r_ref[t]
        w = w_ref[t]
        k = k_ref[t]
        kk = kk_ref[t]
        b = b_ref[t]
        for vi in range(HEAD_DIM):
            sv = s_scr[vi]
            skk = jnp.sum(sv * kk, axis=0, keepdims=True)
            skk = skk + pltpu.roll(skk, CHAINS, 1)
            vrow = v_ref[t, pl.ds(vi, 1), :]
            sv = sv * w - skk * b + vrow * k
            s_scr[vi] = sv
            o = jnp.sum(sv * r, axis=0, keepdims=True)
            o_ref[t, pl.ds(vi, 1), :] = o + pltpu.roll(o, CHAINS, 1)
        return carry

    lax.fori_loop(0, tb, step, 0)

    @pl.when(t_blk == pl.num_programs(1) - 1)
    def _():
        sf_ref[...] = s_scr[...]


def _state_to_chain_layout(s):
    n_seq, h = s.shape[:2]
    g = (n_seq * h) // CHAINS
    s = s.reshape(g, CHAINS, HEAD_DIM, 2, KH).transpose(2, 4, 0, 3, 1)
    return s.reshape(HEAD_DIM, KH, g * LANES)


def _state_from_chain_layout(s, n_seq, h):
    g = (n_seq * h) // CHAINS
    s = s.reshape(HEAD_DIM, KH, g, 2, CHAINS).transpose(2, 4, 0, 3, 1)
    return s.reshape(n_seq, h, HEAD_DIM, HEAD_DIM)


def _rwkv_scan(r, w, k, kk, b, v, s0):
    n_seq, h = s0.shape[:2]
    t = r.shape[0]
    g = (n_seq * h) // CHAINS
    tb = _pick(t, (64, 32, 16, 8, 4, 2, 1))
    ins = [r, w, k, kk, b]
    vin = v
    sin = _state_to_chain_layout(s0)
    kspec = pl.BlockSpec((tb, KH, LANES), lambda gi, ti: (ti, 0, gi))
    vspec = pl.BlockSpec((tb, HEAD_DIM, LANES), lambda gi, ti: (ti, 0, gi))
    sspec = pl.BlockSpec((HEAD_DIM, KH, LANES), lambda gi, ti: (0, 0, gi))
    o, sf = pl.pallas_call(
        functools.partial(_rwkv_scan_kernel, tb=tb),
        grid=(g, t // tb),
        in_specs=[kspec] * 5 + [vspec, sspec],
        out_specs=[vspec, sspec],
        out_shape=[jax.ShapeDtypeStruct((t, HEAD_DIM, g * LANES), F32),
                   jax.ShapeDtypeStruct((HEAD_DIM, KH, g * LANES), F32)],
        scratch_shapes=[pltpu.VMEM((HEAD_DIM, KH, LANES), F32)],
        compiler_params=_cparams(("parallel", "arbitrary")),
        name="rwkv_scan",
    )(*ins, vin, sin)
    return o, _state_from_chain_layout(sf, n_seq, h)


def _rwkv_post_kernel(o_ref, g_ref, bon_ref, gg_ref, gb_ref, out_ref, *, dr):
    n_heads = dr // HEAD_DIM
    sel = _head_select_minor(dr, n_heads)
    exp = _head_expand_minor(n_heads, dr)
    for s, o in enumerate(_chain_to_pair_v(o_ref[...])):
        mu = _head_sum_bcast(o, sel, exp) * (1.0 / HEAD_DIM)
        oc = o - mu
        var = _head_sum_bcast(oc * oc, sel, exp) * (1.0 / HEAD_DIM)
        y = oc * lax.rsqrt(var + GN_EPS) * gg_ref[...] + gb_ref[...]
        out_ref[s] = ((y + bon_ref[s]) * g_ref[s]).astype(BF16)


def _rwkv_post(o, g, bonus, gn_g, gn_b):
    n_seq, t, dr = g.shape
    tm = t if t <= 64 else _pick(t, (64, 32, 16, 8))
    row = pl.BlockSpec((2, tm, dr), lambda p, i: (p, i, 0))
    vec = pl.BlockSpec((1, dr), lambda p, i: (0, 0))
    return pl.pallas_call(
        functools.partial(_rwkv_post_kernel, dr=dr),
        grid=(n_seq // 2, t // tm),
        in_specs=[pl.BlockSpec((tm, HEAD_DIM, LANES), lambda p, i: (i, 0, p)), row, row, vec, vec],
        out_specs=row,
        out_shape=jax.ShapeDtypeStruct((n_seq, t, dr), BF16),
        compiler_params=_cparams(("parallel", "parallel")),
        name="rwkv_post",
    )(o, g, bonus, gn_g.reshape(1, dr), gn_b.reshape(1, dr))


def _logf_kernel(lora_ref, bf_ref, lf_ref, c_ref, carry, *, tb, pad, f_off, n_heads):
    i = pl.program_id(1)

    @pl.when(i == 0)
    def _():
        carry[...] = jnp.zeros_like(carry)

    z = lora_ref[:, f_off:f_off + n_heads] + bf_ref[...]
    lf = -_softplus(-z)
    pos = i * tb + lax.broadcasted_iota(jnp.int32, (tb, 1), 0)
    lf = jnp.where(pos >= pad, lf, 0.0)
    rr = lax.broadcasted_iota(jnp.int32, (tb, tb), 0)
    cc = lax.broadcasted_iota(jnp.int32, (tb, tb), 1)
    tril = jnp.where(cc <= rr, 1.0, 0.0).astype(BF16)
    c = _dot_01x(tril, lf) + carry[0:1, :]
    lf_ref[...] = lf
    c_ref[...] = c
    carry[0:1, :] = c[tb - 1:tb, :]


def _logf_cumsum(p, b_f, prm, n_seq, seq_rows, pad):
    dl = prm["lora_w"]
    hf = b_f.shape[-1]
    tb = _pick(seq_rows, (128, 64, 32, 16, 8))
    lora_blk = (3 * prm["dr"] + 3 * prm["df"]) // dl
    p3 = p.reshape(n_seq, seq_rows, p.shape[-1])
    out = pl.BlockSpec((None, tb, hf), lambda s, i: (s, i, 0))
    return pl.pallas_call(
        functools.partial(_logf_kernel, tb=tb, pad=pad, f_off=prm["f_off"], n_heads=hf),
        grid=(n_seq, seq_rows // tb),
        in_specs=[pl.BlockSpec((None, tb, dl), lambda s, i: (s, i, lora_blk)),
                  pl.BlockSpec((1, hf), lambda s, i: (0, 0))],
        out_specs=[out, out],
        out_shape=[jax.ShapeDtypeStruct((n_seq, seq_rows, hf), F32)] * 2,
        scratch_shapes=[pltpu.VMEM((8, hf), F32)],
        compiler_params=_cparams(("parallel", "arbitrary")),
        name="logf_cumsum",
    )(p3, b_f.reshape(1, hf))


LOG2E = 1.4426950408889634


def _bf16_terms(x):
    hi = x.astype(BF16)
    r = x - hi.astype(F32)
    mid = r.astype(BF16)
    lo = (r - mid.astype(F32)).astype(BF16)
    return hi.astype(F32), mid.astype(F32), lo.astype(F32)


def _augment(x, own, lane, first, ones_at, terms):
    t0, t1, t2 = terms
    a = first
    extra = jnp.where(lane == a + (3 - ones_at), t0, jnp.where(lane == a + (4 - ones_at), t1,
            jnp.where(lane == a + (5 - ones_at), t2, 0.0)))
    extra = jnp.where((lane >= a + ones_at) & (lane < a + ones_at + 3), 1.0, extra)
    return jnp.where(own, x, extra).astype(BF16)


def _fox_prompt_kernel(q_ref, k_ref, v_ref, cq_ref, ck_ref, o_ref, ka_scr, va_scr, m_scr, acc_scr, *, tq, scale2):
    i = pl.program_id(2)
    seq_rows = k_ref.shape[0]

    @pl.when(i == 0)
    def _():
        lane_k = lax.broadcasted_iota(jnp.int32, (seq_rows, LANES), 1)
        k = k_ref[...]
        v = v_ref[...]
        for h in range(2):
            own = (lane_k < HEAD_DIM) if h == 0 else (lane_k >= HEAD_DIM)
            first = HEAD_DIM * (1 - h)
            ka_scr[h] = _augment(k, own, lane_k, first, 0, _bf16_terms(-ck_ref[:, h:h + 1]))
            va_scr[h] = jnp.where(own, v, 1.0).astype(BF16)

    lane = lax.broadcasted_iota(jnp.int32, (tq, LANES), 1)
    q = q_ref[...] * scale2
    qa = []
    for h in range(2):
        own = (lane < HEAD_DIM) if h == 0 else (lane >= HEAD_DIM)
        qa.append(_augment(q, own, lane, HEAD_DIM * (1 - h), 3, _bf16_terms(cq_ref[:, h:h + 1])))
    m_scr[...] = jnp.full_like(m_scr, NEG_INF)
    acc_scr[...] = jnp.zeros_like(acc_scr)

    def block(j, width, diagonal):
        off = pl.multiple_of(j * tq, tq)
        for h in range(2):
            s = _dot_nt(qa[h], ka_scr[h, pl.ds(off, width), :])
            if diagonal:
                rr = lax.broadcasted_iota(jnp.int32, (tq, width), 0)
                cc = lax.broadcasted_iota(jnp.int32, (tq, width), 1)
                s = jnp.where(cc <= rr, s, NEG_INF)
            cols = [s[:, c * LANES:(c + 1) * LANES] for c in range(width // LANES)]
            m_blk = jnp.max(functools.reduce(jnp.maximum, cols), axis=-1, keepdims=True)
            m_prev = m_scr[h]
            m_new = jnp.maximum(m_prev, m_blk)
            p = jnp.concatenate([jnp.exp2(c - m_new) for c in cols], axis=1).astype(BF16)
            acc_scr[h] = jnp.exp2(m_prev - m_new) * acc_scr[h] + _dot(p, va_scr[h, pl.ds(off, width), :])
            m_scr[h] = m_new

    def body(jj, carry):
        block(2 * jj, 2 * tq, False)
        return carry

    lax.fori_loop(0, i // 2, body, 0)

    @pl.when(i % 2 == 1)
    def _():
        block(i - 1, tq, False)

    block(i, tq, True)
    a0 = acc_scr[0]
    a1 = acc_scr[1]
    o = jnp.where(lane < HEAD_DIM, a0 / pltpu.roll(a0, HEAD_DIM, 1), a1 / pltpu.roll(a1, HEAD_DIM, 1))
    o_ref[...] = o.astype(BF16)


def _fox_prompt(p, c, prm, n_seq, seq_rows, pad):
    dr, df = prm["dr"], prm["df"]
    hf = df // HEAD_DIM
    n_pairs = hf // 2
    tq = _pick(seq_rows, (384, 256, 128))
    nq = seq_rows // tq
    p3 = p.reshape(n_seq, seq_rows, p.shape[-1])
    c4 = (c * LOG2E).reshape(n_seq, seq_rows, n_pairs, 2)
    cq = c4.transpose(0, 2, 1, 3)
    is_pad = (jnp.arange(seq_rows) < pad)[None, :, None, None]
    ck = jnp.where(is_pad, -NEG_INF, c4).transpose(0, 2, 1, 3)
    q0, k0, v0 = (3 * dr) // LANES, (3 * dr + df) // LANES, (3 * dr + 2 * df) // LANES
    return pl.pallas_call(
        functools.partial(_fox_prompt_kernel, tq=tq, scale2=HEAD_DIM ** -0.5 * LOG2E),
        grid=(n_seq, n_pairs, nq),
        in_specs=[pl.BlockSpec((None, tq, LANES), lambda b, pr, i: (b, i, q0 + pr)),
                  pl.BlockSpec((None, seq_rows, LANES), lambda b, pr, i: (b, 0, k0 + pr)),
                  pl.BlockSpec((None, seq_rows, LANES), lambda b, pr, i: (b, 0, v0 + pr)),
                  pl.BlockSpec((None, None, tq, 2), lambda b, pr, i: (b, pr, i, 0)),
                  pl.BlockSpec((None, None, seq_rows, 2), lambda b, pr, i: (b, pr, 0, 0))],
        out_specs=pl.BlockSpec((None, tq, LANES), lambda b, pr, i: (b, i, pr)),
        out_shape=jax.ShapeDtypeStruct((n_seq, seq_rows, df), BF16),
        scratch_shapes=[pltpu.VMEM((2, seq_rows, LANES), BF16), pltpu.VMEM((2, seq_rows, LANES), BF16),
                        pltpu.VMEM((2, tq, LANES), F32), pltpu.VMEM((2, tq, LANES), F32)],
        compiler_params=_cparams(("parallel", "parallel", "arbitrary")),
        name="fox_prompt",
    )(p3, p3, p3, cq, ck).reshape(n_seq * seq_rows, df)


def _fox_sample_kernel(pt_ref, q_ref, kn_ref, vn_ref, cn_ref, cnrow_ref, *rest, pp, page, nq, hf, scale):
    lf_refs = rest[:pp]
    k_refs = rest[pp:2 * pp]
    v_refs = rest[2 * pp:3 * pp]
    o_ref, qbd_scr, m_scr, l_scr, acc_scr, carry_scr = rest[3 * pp:]
    step = pl.program_id(1)
    gh = MXU_DEPTH // HEAD_DIM
    ng = hf // gh
    gr = gh * nq
    gd = gh * HEAD_DIM
    rows = hf * nq
    own = (_idiv(lax.broadcasted_iota(jnp.int32, (gr, gd), 0), nq)
           == _idiv(lax.broadcasted_iota(jnp.int32, (gr, gd), 1), HEAD_DIM))

    @pl.when(step == 0)
    def _():
        q = q_ref[...] * scale
        for g in range(ng):
            qg = jnp.broadcast_to(q[None, :, g * gd:(g + 1) * gd], (gh, nq, gd)).reshape(gr, gd)
            qbd_scr[g] = jnp.where(own, qg, 0.0).astype(BF16)
        m_scr[...] = jnp.full_like(m_scr, NEG_INF)
        l_scr[...] = jnp.zeros_like(l_scr)
        acc_scr[...] = jnp.zeros_like(acc_scr)
        carry_scr[...] = jnp.zeros_like(carry_scr)

    tr = lax.broadcasted_iota(jnp.int32, (page, 2 * page), 0)
    tc = lax.broadcasted_iota(jnp.int32, (page, 2 * page), 1)
    suffix = jnp.where((tc >= page) | (tr > tc), 1.0, 0.0).astype(BF16)
    carry = carry_scr[...]
    later_sum = [None] * pp
    for x in reversed(range(pp)):
        r = _dot_x01(lf_refs[x][...], suffix)
        later_sum[x] = r[:, :page] + carry
        carry = carry + r[:, page:]
    carry_scr[...] = carry
    bias = jnp.concatenate(later_sum, axis=1)
    bias = jnp.broadcast_to(bias[:, None, :], (hf, nq, pp * page)).reshape(rows, pp * page)

    group = lambda ref, g: ref[g * gd:(g + 1) * gd, :].astype(BF16)
    s = jnp.concatenate(
        [jnp.concatenate([_dot(qbd_scr[g], group(k_refs[x], g)) for g in range(ng)], axis=0) for x in range(pp)],
        axis=1)
    s = s + cn_ref[...] + bias
    m_prev = m_scr[...]
    m_new = jnp.maximum(m_prev, jnp.max(s, axis=-1, keepdims=True))
    alpha = jnp.exp(m_prev - m_new)
    p = jnp.exp(s - m_new)
    l_scr[...] = alpha * l_scr[...] + jnp.sum(p, axis=-1, keepdims=True)
    m_scr[...] = m_new
    for g in range(ng):
        rws = slice(g * gr, (g + 1) * gr)
        acc = alpha[rws] * acc_scr[g]
        for x in range(pp):
            acc = acc + _dot_nt(p[rws, x * page:(x + 1) * page].astype(BF16), group(v_refs[x], g))
        acc_scr[g] = acc

    @pl.when(step == pl.num_programs(1) - 1)
    def _():
        q = (q_ref[...] * scale).astype(BF16)
        kn = kn_ref[...].astype(BF16)
        vn = vn_ref[...].astype(BF16)
        head = lambda a, h: a[:, h * HEAD_DIM:(h + 1) * HEAD_DIM]
        sn = jnp.stack([_dot_nt(head(q, h), head(kn, h)) for h in range(hf)])
        qi = lax.broadcasted_iota(jnp.int32, (hf, nq, nq), 1)
        ki = lax.broadcasted_iota(jnp.int32, (hf, nq, nq), 2)
        sn = jnp.where(ki <= qi, sn + cnrow_ref[...], NEG_INF).reshape(rows, nq)
        m_last = m_scr[...]
        m_fin = jnp.maximum(m_last, jnp.max(sn, axis=-1, keepdims=True))
        a_fin = jnp.exp(m_last - m_fin)
        pn = jnp.exp(sn - m_fin)
        inv = 1.0 / (a_fin * l_scr[...] + jnp.sum(pn, axis=-1, keepdims=True))
        pn = (pn * inv).astype(BF16)
        o_new = jnp.concatenate([_dot(pn[h * nq:(h + 1) * nq, :], head(vn, h)) for h in range(hf)], axis=1)
        w_past = a_fin * inv
        o_past = jnp.concatenate(
            [jnp.sum(jnp.where(own, acc_scr[g] * w_past[g * gr:(g + 1) * gr], 0.0).reshape(gh, nq, gd), axis=0)
             for g in range(ng)], axis=1)
        o_ref[...] = (o_past + o_new).astype(BF16)


def _fox_sample(p_s, c_new, cache_k, cache_v, cache_lft, page_table_flat, prm, n_seq, nq, n_pages, layer):
    dr, df = prm["dr"], prm["df"]
    hf = df // HEAD_DIM
    page = cache_k.shape[3]
    gh = MXU_DEPTH // HEAD_DIM
    assert hf % gh == 0
    pp = _pick(n_pages, (4, 2, 1))
    npb = n_pages // pp
    rows = hf * nq
    p3 = p_s.reshape(n_seq, nq, p_s.shape[-1])
    cn_t = c_new.transpose(0, 2, 1)
    cn = cn_t.reshape(n_seq, rows, 1)
    cn_row = cn_t[..., None] - cn_t[:, :, None, :]
    qb, kb, vb = (3 * dr) // df, (3 * dr + df) // df, (3 * dr + 2 * df) // df
    seq_blk = lambda blk: pl.BlockSpec((None, nq, df), lambda b, i, pt: (b, 0, blk))

    def page_of(b, i, pt, x):
        return pt[b * n_pages + (npb - 1 - i) * pp + x]

    def kv_spec(x):
        return pl.BlockSpec((None, None, df, page), lambda b, i, pt: (layer, page_of(b, i, pt, x), 0, 0))

    def lf_spec(x):
        return pl.BlockSpec((None, None, hf, page), lambda b, i, pt: (layer, page_of(b, i, pt, x), 0, 0))

    grid_spec = pltpu.PrefetchScalarGridSpec(
        num_scalar_prefetch=1,
        grid=(n_seq, npb),
        in_specs=[seq_blk(qb), seq_blk(kb), seq_blk(vb),
                  pl.BlockSpec((None, rows, 1), lambda b, i, pt: (b, 0, 0)),
                  pl.BlockSpec((None, hf, nq, nq), lambda b, i, pt: (b, 0, 0, 0))]
                 + [lf_spec(x) for x in range(pp)] + [kv_spec(x) for x in range(pp)] * 2,
        out_specs=pl.BlockSpec((None, nq, df), lambda b, i, pt: (b, 0, 0)),
        scratch_shapes=[pltpu.VMEM((hf // gh, gh * nq, MXU_DEPTH), BF16), pltpu.VMEM((rows, 1), F32),
                        pltpu.VMEM((rows, 1), F32), pltpu.VMEM((hf // gh, gh * nq, MXU_DEPTH), F32),
                        pltpu.VMEM((hf, page), F32)],
    )
    out = pl.pallas_call(
        functools.partial(_fox_sample_kernel, pp=pp, page=page, nq=nq, hf=hf, scale=HEAD_DIM ** -0.5),
        grid_spec=grid_spec,
        out_shape=jax.ShapeDtypeStruct((n_seq, nq, df), BF16),
        compiler_params=_cparams(("parallel", "arbitrary")),
        name="fox_sample",
    )(page_table_flat, p3, p3, p3, cn, cn_row, *([cache_lft] * pp), *([cache_k] * pp), *([cache_v] * pp))
    return out.reshape(n_seq * nq, df)


def _gelu_tanh(x):
    return 0.5 * x * (1.0 + jnp.tanh(0.7978845608028654 * (x + 0.044715 * (x * x * x))))


def _ffn_in_kernel(*refs, tm, tn, seq_rows, pad, sample):
    if sample:
        x_ref, wu_ref, wg_ref, cw_ref, cb_ref, p1_ref, p2_ref, h_ref, u_ref = refs
    else:
        x_ref, wu_ref, wg_ref, cw_ref, cb_ref, h_ref, tail_ref, carry = refs
    i = pl.program_id(0)
    j = pl.program_id(1)
    x = x_ref[...]
    u = _dot(x, wu_ref[...])
    gate = _dot(x, wg_ref[...])
    row = lax.broadcasted_iota(jnp.int32, (tm, 1), 0)
    if sample:
        t = _imod(row, seq_rows)
        u1 = jnp.where(t == 0, p1_ref[...], pltpu.roll(u, 1, 0))
        u2 = jnp.where(t < 2, p2_ref[...], pltpu.roll(u, 2, 0))
        u_ref[...] = u
    else:
        @pl.when(i == 0)
        def _():
            carry[:, pl.ds(pl.multiple_of(j * tn, tn), tn)] = jnp.zeros((8, tn), F32)

        pos = (i * tm) % seq_rows + row
        u = jnp.where(pos >= pad, u, 0.0)
        prev = carry[:, pl.ds(pl.multiple_of(j * tn, tn), tn)]
        u1 = jnp.where(row == 0, prev[7:8, :], pltpu.roll(u, 1, 0))
        u2 = jnp.where(row == 0, prev[6:7, :], jnp.where(row == 1, prev[7:8, :], pltpu.roll(u, 2, 0)))
        tail = u[tm - 8:tm, :]
        carry[:, pl.ds(pl.multiple_of(j * tn, tn), tn)] = tail
        tail_ref[...] = tail
    c = cb_ref[...] + cw_ref[0:1, :] * u2 + cw_ref[1:2, :] * u1 + cw_ref[2:3, :] * u
    h_ref[...] = (_gelu_tanh(c) * gate).astype(BF16)


def _ffn_in(x1b, w_ffn_in_b, conv_w, conv_b, p1, p2, *, seq_rows, pad, sample):
    m, d = x1b.shape
    dff = conv_w.shape[-1]
    tn = _pick(dff, (256, 128))
    nj = dff // tn
    tm = m if sample else _pick(seq_rows, (1056, 1024, 512, 384, 256, 128, 64, 32, 16, 8))
    ni = m // tm
    in_specs = [pl.BlockSpec((tm, d), lambda i, j: (i, 0)),
                pl.BlockSpec((d, tn), lambda i, j: (0, j)),
                pl.BlockSpec((d, tn), lambda i, j: (0, j + nj)),
                pl.BlockSpec((3, tn), lambda i, j: (0, j)),
                pl.BlockSpec((1, tn), lambda i, j: (0, j))]
    args = [x1b, w_ffn_in_b, w_ffn_in_b, conv_w, conv_b.reshape(1, dff)]
    tile = pl.BlockSpec((tm, tn), lambda i, j: (i, j))
    if sample:
        in_specs += [tile, tile]
        args += [p1, p2]
        out_specs = [tile, tile]
        out_shape = [jax.ShapeDtypeStruct((m, dff), BF16), jax.ShapeDtypeStruct((m, dff), F32)]
        scratch = []
    else:
        out_specs = [tile, pl.BlockSpec((8, tn), lambda i, j: (i, j))]
        out_shape = [jax.ShapeDtypeStruct((m, dff), BF16), jax.ShapeDtypeStruct((ni * 8, dff), F32)]
        scratch = [pltpu.VMEM((8, dff), F32)]
    return pl.pallas_call(
        functools.partial(_ffn_in_kernel, tm=tm, tn=tn, seq_rows=seq_rows, pad=pad, sample=sample),
        grid=(ni, nj),
        in_specs=in_specs,
        out_specs=out_specs,
        out_shape=out_shape,
        scratch_shapes=scratch,
        compiler_params=_cparams(("arbitrary", "arbitrary")),
        name="ffn_in_sample" if sample else "ffn_in_prompt",
    )(*args)


def _layer_params(lw, d_model):
    (w_in, b_f, mu_shift, w0, w_decay_up, a0, w_iclr_up, w_gate_up, k_k, k_a, r_k,
     gn_g, gn_b, w_out, ln1_g, ln1_b, w_ffn_in, conv_w, conv_b, w_ffn_out, ln2_g, ln2_b) = lw
    dr = w0.shape[-1]
    hf = b_f.shape[-1]
    df = hf * HEAD_DIM
    d_dec, d_icl, d_gate = w_decay_up.shape[0], w_iclr_up.shape[0], w_gate_up.shape[0]
    rw_cols = 3 * dr + d_dec + d_icl + d_gate
    n_lora = d_dec + d_icl + d_gate
    lora_w = -(-(n_lora + hf) // (4 * LANES)) * (4 * LANES)
    assert (3 * dr + 3 * df) % lora_w == 0 and dr % LANES == 0 and df % LANES == 0
    pk, pv = _rwkv_col_orders(dr // HEAD_DIM)
    on_k = lambda a: a[..., pk]
    on_v = lambda a: a[..., pv]
    fox0 = rw_cols
    w_perm = jnp.concatenate(
        [on_k(w_in[:, :dr]), on_k(w_in[:, dr:2 * dr]), on_v(w_in[:, 2 * dr:3 * dr]),
         w_in[:, fox0:fox0 + 3 * df], w_in[:, 3 * dr:rw_cols],
         w_in[:, fox0 + 3 * df:], jnp.zeros((d_model, lora_w - n_lora - hf), w_in.dtype)], axis=1).astype(BF16)

    def pad_rows(w, off):
        return jnp.zeros((lora_w, dr), F32).at[off:off + w.shape[0]].set(w).astype(BF16)

    w_out_perm = jnp.concatenate([w_out[:dr][pv], w_out[dr:]], axis=0).astype(BF16)
    prm = dict(
        dr=dr, df=df, lora_w=lora_w, d_dec=d_dec, d_icl=d_icl, f_off=n_lora, rw_cols=rw_cols, pk=pk, pv=pv,
        mu_rkv=jnp.concatenate([on_k(mu_shift[:dr]), on_k(mu_shift[dr:2 * dr]),
                                on_v(mu_shift[2 * dr:3 * dr])]).reshape(1, -1),
        mu_lora=jnp.zeros((1, lora_w), F32).at[0, :n_lora].set(mu_shift[3 * dr:]),
        wd=pad_rows(on_k(w_decay_up), 0), wa=pad_rows(on_k(w_iclr_up), d_dec),
        wg=pad_rows(on_v(w_gate_up), d_dec + d_icl),
        w0=on_k(w0).reshape(1, dr), a0=on_k(a0).reshape(1, dr), k_k=on_k(k_k).reshape(1, dr),
        k_a=on_k(k_a).reshape(1, dr), r_k=on_k(r_k.reshape(dr)).reshape(1, dr), w_perm=w_perm,
        w_out=w_out_perm, w_ffn_in=w_ffn_in.astype(BF16), w_ffn_out=w_ffn_out.astype(BF16), b_f=b_f,
        gn_g=on_v(gn_g), gn_b=on_v(gn_b), ln1=(ln1_g, ln1_b), ln2=(ln2_g, ln2_b), conv_w=conv_w, conv_b=conv_b)
    return prm


def _mm_tiles(m, n, seq_rows):
    tm = _pick(seq_rows, (1056, 1024, 768, 512, 384, 256, 128, 64, 32, 16, 8)) if m > 512 else m
    tn = _pick(n, (1024, 512, 256, 128))
    return tm, tn


def _permute_rw_row(row, prm):
    dr = prm["dr"]
    n_lora = prm["f_off"]
    lora = jnp.zeros(row.shape[:-1] + (prm["lora_w"],), F32).at[..., :n_lora].set(row[..., 3 * dr:])
    rkv = jnp.concatenate([row[..., :dr][..., prm["pk"]], row[..., dr:2 * dr][..., prm["pk"]],
                           row[..., 2 * dr:3 * dr][..., prm["pv"]]], axis=-1)
    return rkv, lora


def _unpermute_rkv(rkv, prm):
    dr = prm["dr"]
    ik, iv = np.argsort(prm["pk"]), np.argsort(prm["pv"])
    return jnp.concatenate([rkv[..., :dr][..., ik], rkv[..., dr:2 * dr][..., ik],
                            rkv[..., 2 * dr:3 * dr][..., iv]], axis=-1)


def _run_group(x2, ln0, prm, alpha, *, n_seq, seq_rows, pad, skip, sample, shift_prev, s0, ffn_prev, attend):
    ln0_g, ln0_b = ln0
    m, d = x2.shape
    dr, df = prm["dr"], prm["df"]
    h_b = _ln0(x2, ln0_g, ln0_b)
    tm, tn = _mm_tiles(m, prm["w_perm"].shape[1], seq_rows)
    p = _matmul(h_b, prm["w_perm"], tm, tn, "in_proj")

    p3 = p.reshape(n_seq, seq_rows, -1)
    if sample:
        sp_rkv, sp_lora = _permute_rw_row(shift_prev, prm)
        expand = lambda a: jnp.zeros((n_seq, seq_rows, a.shape[-1]), F32).at[:, 0].set(a)
        sp_rkv, sp_lora = expand(sp_rkv), expand(sp_lora)
    else:
        sp_rkv = sp_lora = None
    r, w, k, v, kk, b, g, bonus = _rwkv_prep(p3, sp_rkv, sp_lora, prm, pad=pad, sample=sample)
    o_scan, s_last = _rwkv_scan(r, w, k, kk, b, v, s0)
    o_rw = _rwkv_post(o_scan, g, bonus, prm["gn_g"], prm["gn_b"]).reshape(m, dr)

    lf, c = _logf_cumsum(p, prm["b_f"], prm, n_seq, seq_rows, pad)
    o_fox = attend(p, c)

    tm, tn = _mm_tiles(m, d, seq_rows)
    mix = _matmul2(o_rw, o_fox, prm["w_out"], tm, tn, "out_proj")
    x1, x1b = _ln1(x2, mix, ln0_g, ln0_b, *prm["ln1"], alpha)

    if sample:
        p1 = jnp.zeros((n_seq, seq_rows, ffn_prev.shape[-1]), F32).at[:, 0].set(ffn_prev[:, 1])
        p2 = p1.at[:, 0].set(ffn_prev[:, 0]).at[:, 1].set(ffn_prev[:, 1])
        p1, p2 = p1.reshape(m, -1), p2.reshape(m, -1)
    else:
        p1 = p2 = None
    hid, u_aux = _ffn_in(x1b, prm["w_ffn_in"], prm["conv_w"], prm["conv_b"], p1, p2,
                         seq_rows=seq_rows, pad=pad, sample=sample)
    dff = hid.shape[1]
    tm_o = _pick(seq_rows, (528, 512, 384, 256, 128, 64, 32, 16, 8)) if m > 512 else m
    ffn = _matmul(hid, prm["w_ffn_out"], tm_o, _pick(d, (512, 256, 128)), "ffn_out")
    y = _ln2(x1, ffn, *prm["ln2"], alpha, n_seq, seq_rows, skip)

    hf = df // HEAD_DIM
    k_new = p3[:, pad:, 3 * dr + df:3 * dr + 2 * df].reshape(n_seq, seq_rows - pad, hf, HEAD_DIM)
    v_new = p3[:, pad:, 3 * dr + 2 * df:3 * dr + 3 * df].reshape(n_seq, seq_rows - pad, hf, HEAD_DIM)
    last = p3[:, -1]
    shift_last = jnp.concatenate([_unpermute_rkv(last[:, :3 * dr], prm),
                                  last[:, 3 * dr + 3 * df:3 * dr + 3 * df + prm["f_off"]]], axis=1)
    if sample:
        conv_state = u_aux.reshape(n_seq, seq_rows, dff)[:, -2:]
    else:
        tiles_per_seq = u_aux.shape[0] // 8 // n_seq
        conv_state = u_aux.reshape(n_seq, tiles_per_seq, 8, dff)[:, -1, -2:]
    return y, (k_new, v_new, lf[:, pad:], s_last, shift_last, conv_state)


def kernel(x_prompt, x_sample, cache_k, cache_v, cache_logf, state_rwkv, state_shift, state_ffn_conv,
           page_table, meta_tokens, ln0_g, ln0_b, w_in, b_f, mu_shift, w0, w_decay_up, a0, w_iclr_up,
           w_gate_up, k_k, k_a, r_k, gn_g, gn_b, w_out, ln1_g, ln1_b, w_ffn_in, conv_w, conv_b,
           w_ffn_out, ln2_g, ln2_b):
    depth = w_in.shape[0]
    assert depth == 1, "the token layout between layers is only wired for a single layer"
    b, seq, d = x_prompt.shape
    db, nq, _ = x_sample.shape
    n_meta = meta_tokens.shape[0]
    alpha = (2 * depth) ** 0.25
    n_pool, page = cache_k.shape[1:3]
    n_pages = page_table.shape[1]
    h_rw = r_k.shape[1]

    t_real = seq + n_meta
    pad = (-t_real) % Q_BLOCK
    t_pad = t_real + pad
    meta = jnp.broadcast_to(meta_tokens[None].astype(x_prompt.dtype), (b, n_meta, d))
    xp = jnp.concatenate([jnp.zeros((b, pad, d), x_prompt.dtype), meta, x_prompt], axis=1).reshape(b * t_pad, d)
    xs = x_sample.reshape(db * nq, d)
    pt_flat = page_table.reshape(-1).astype(jnp.int32)

    outs_p, outs_s = [], []
    for l in range(depth):
        lw = tuple(wt[l] for wt in (w_in, b_f, mu_shift, w0, w_decay_up, a0, w_iclr_up, w_gate_up, k_k, k_a,
                                     r_k, gn_g, gn_b, w_out, ln1_g, ln1_b, w_ffn_in, conv_w, conv_b,
                                     w_ffn_out, ln2_g, ln2_b))
        prm = _layer_params(lw, d)
        hf = prm["df"] // HEAD_DIM

        attend_p = lambda p, c: _fox_prompt(p, c, prm, b, t_pad, pad)
        y_p, st_p = _run_group(xp, (ln0_g, ln0_b), prm, alpha, n_seq=b, seq_rows=t_pad, pad=pad, skip=pad + n_meta, sample=False,
                               shift_prev=None, s0=jnp.zeros((b, h_rw, HEAD_DIM, HEAD_DIM), F32),
                               ffn_prev=None, attend=attend_p)

        cache_kt = cache_k.transpose(0, 1, 3, 4, 2).reshape(depth, n_pool, hf * HEAD_DIM, page)
        cache_vt = cache_v.transpose(0, 1, 3, 4, 2).reshape(depth, n_pool, hf * HEAD_DIM, page)
        cache_lft = cache_logf.astype(F32).transpose(0, 1, 3, 2)
        attend_s = lambda p, c: _fox_sample(p, c, cache_kt, cache_vt, cache_lft, pt_flat, prm, db, nq, n_pages, l)
        y_s, st_s = _run_group(xs, (ln0_g, ln0_b), prm, alpha, n_seq=db, seq_rows=nq, pad=0, skip=0, sample=True,
                               shift_prev=state_shift[l], s0=state_rwkv[l], ffn_prev=state_ffn_conv[l],
                               attend=attend_s)
        outs_p.append(st_p)
        outs_s.append(st_s)

    k_p, v_p, lf_p, rw_p, sh_p, cv_p = (jnp.stack([o[i] for o in outs_p]) for i in range(6))
    k_s, v_s, lf_s, rw_s, sh_s, cv_s = (jnp.stack([o[i] for o in outs_s]) for i in range(6))
    y_prompt = y_p
    y_sample = y_s.reshape(db, nq, d)
    return (y_prompt, y_sample, k_p, v_p, lf_p, rw_p, sh_p, cv_p, k_s, v_s, lf_s, rw_s, sh_s, cv_s)
```

```python
import functools

import jax
import jax.numpy as jnp
import numpy as np
from jax import lax
from jax.experimental import pallas as pl
from jax.experimental.pallas import tpu as pltpu

F32 = jnp.float32
BF16 = jnp.bfloat16

HEAD_DIM = 64
LANES = 128
SUBLANES = 8
MXU_DEPTH = 256
Q_BLOCK = 128
LN_EPS = 1e-5
GN_EPS = 64e-5
NEG_INF = -1e30
VMEM_LIMIT = 56 * 1024 * 1024


def _cparams(sem):
    return pltpu.CompilerParams(dimension_semantics=sem, vmem_limit_bytes=VMEM_LIMIT)


def _pick(n, candidates):
    for c in candidates:
        if n % c == 0:
            return c
    return n


def _split_bf16(x):
    hi = x.astype(BF16)
    lo = (x - hi.astype(F32)).astype(BF16)
    return hi, lo


def _dot(a, b):
    return jnp.dot(a, b, preferred_element_type=F32)


def _dot_nt(a, b):
    return lax.dot_general(a, b, (((1,), (1,)), ((), ())), preferred_element_type=F32)


def _dot_x01(x, m01):
    hi, lo = _split_bf16(x)
    return _dot(hi, m01) + _dot(lo, m01)


def _dot_01x(m01, x):
    hi, lo = _split_bf16(x)
    return _dot(m01, hi) + _dot(m01, lo)


def _idiv(x, c):
    return lax.shift_right_logical(x, c.bit_length() - 1) if c & (c - 1) == 0 else x // c


def _imod(x, c):
    return (x & (c - 1)) if c & (c - 1) == 0 else x % c


def _head_sum_bcast(x, sel, exp):
    return _dot_x01(_dot_x01(x, sel), exp)


def _softplus(z):
    return jnp.maximum(z, 0.0) + jnp.log1p(jnp.exp(-jnp.abs(z)))


def _sigmoid(z):
    return 1.0 / (1.0 + jnp.exp(-z))


def _ln_rows(x, g, b):
    mu = jnp.mean(x, axis=-1, keepdims=True)
    xc = x - mu
    var = jnp.mean(xc * xc, axis=-1, keepdims=True)
    return xc * lax.rsqrt(var + LN_EPS) * g + b


def _ln0_kernel(x_ref, g_ref, b_ref, o_ref):
    o_ref[...] = _ln_rows(x_ref[...], g_ref[...], b_ref[...]).astype(o_ref.dtype)


def _ln0(x2, g, b):
    m, d = x2.shape
    tm = _pick(m, (528, 512, 384, 256, 128, 64, 32, 16, 8))
    return pl.pallas_call(
        _ln0_kernel,
        grid=(m // tm,),
        in_specs=[pl.BlockSpec((tm, d), lambda i: (i, 0)),
                  pl.BlockSpec((1, d), lambda i: (0, 0)),
                  pl.BlockSpec((1, d), lambda i: (0, 0))],
        out_specs=pl.BlockSpec((tm, d), lambda i: (i, 0)),
        out_shape=jax.ShapeDtypeStruct((m, d), BF16),
        compiler_params=_cparams(("parallel",)),
        name="ln0",
    )(x2, g.reshape(1, d), b.reshape(1, d))


def _ln1_kernel(x_ref, mix_ref, g0_ref, b0_ref, g_ref, b_ref, of_ref, ob_ref, *, alpha):
    h = _ln_rows(x_ref[...], g0_ref[...], b0_ref[...])
    x1 = _ln_rows(alpha * h + mix_ref[...], g_ref[...], b_ref[...])
    of_ref[...] = x1
    ob_ref[...] = x1.astype(BF16)


def _ln1(x2, mix, g0, b0, g, b, alpha):
    m, d = x2.shape
    tm = _pick(m, (264, 256, 128, 64, 32, 16, 8))
    row = pl.BlockSpec((tm, d), lambda i: (i, 0))
    vec = pl.BlockSpec((1, d), lambda i: (0, 0))
    return pl.pallas_call(
        functools.partial(_ln1_kernel, alpha=alpha),
        grid=(m // tm,),
        in_specs=[row, row, vec, vec, vec, vec],
        out_specs=[row, row],
        out_shape=[jax.ShapeDtypeStruct((m, d), F32), jax.ShapeDtypeStruct((m, d), BF16)],
        compiler_params=_cparams(("parallel",)),
        name="ln1",
    )(x2, mix, g0.reshape(1, d), b0.reshape(1, d), g.reshape(1, d), b.reshape(1, d))


def _ln2_kernel(x1_ref, ffn_ref, g_ref, b_ref, o_ref, *, alpha):
    o_ref[...] = _ln_rows(alpha * x1_ref[...] + ffn_ref[...], g_ref[...], b_ref[...])


def _ln2(x1, ffn, g, b, alpha, n_seq, seq_rows, skip_rows):
    d = x1.shape[-1]
    out_rows = seq_rows - skip_rows
    tb = next(c for c in (256, 128, 64, 32, 16, 8) if skip_rows % c == 0 and out_rows % c == 0)
    skip = skip_rows // tb
    x3 = x1.reshape(n_seq, seq_rows, d)
    f3 = ffn.reshape(n_seq, seq_rows, d)
    row_in = pl.BlockSpec((None, tb, d), lambda s, i: (s, i + skip, 0))
    vec = pl.BlockSpec((1, d), lambda s, i: (0, 0))
    return pl.pallas_call(
        functools.partial(_ln2_kernel, alpha=alpha),
        grid=(n_seq, out_rows // tb),
        in_specs=[row_in, row_in, vec, vec],
        out_specs=pl.BlockSpec((None, tb, d), lambda s, i: (s, i, 0)),
        out_shape=jax.ShapeDtypeStruct((n_seq, out_rows, d), F32),
        compiler_params=_cparams(("parallel", "parallel")),
        name="ln2",
    )(x3, f3, g.reshape(1, d), b.reshape(1, d))


def _mm_kernel(x_ref, w_ref, o_ref):
    o_ref[...] = _dot(x_ref[...], w_ref[...])


def _matmul(x, w, tm, tn, name):
    m, k = x.shape
    n = w.shape[1]
    return pl.pallas_call(
        _mm_kernel,
        grid=(m // tm, n // tn),
        in_specs=[pl.BlockSpec((tm, k), lambda i, j: (i, 0)),
                  pl.BlockSpec((k, tn), lambda i, j: (0, j))],
        out_specs=pl.BlockSpec((tm, tn), lambda i, j: (i, j)),
        out_shape=jax.ShapeDtypeStruct((m, n), F32),
        compiler_params=_cparams(("parallel", "arbitrary")),
        name=name,
    )(x, w)


def _mm2_kernel(xa_ref, xb_ref, wa_ref, wb_ref, o_ref):
    o_ref[...] = _dot(xa_ref[...], wa_ref[...]) + _dot(xb_ref[...], wb_ref[...])


def _matmul2(xa, xb, w, tm, tn, name):
    m, ka = xa.shape
    kb = xb.shape[1]
    n = w.shape[1]
    assert ka == kb, "the weight's second row block is addressed as block index 1"
    return pl.pallas_call(
        _mm2_kernel,
        grid=(m // tm, n // tn),
        in_specs=[pl.BlockSpec((tm, ka), lambda i, j: (i, 0)),
                  pl.BlockSpec((tm, kb), lambda i, j: (i, 0)),
                  pl.BlockSpec((ka, tn), lambda i, j: (0, j)),
                  pl.BlockSpec((kb, tn), lambda i, j: (1, j))],
        out_specs=pl.BlockSpec((tm, tn), lambda i, j: (i, j)),
        out_shape=jax.ShapeDtypeStruct((m, n), F32),
        compiler_params=_cparams(("parallel", "arbitrary")),
        name=name,
    )(xa, xb, w, w)


CHAINS = LANES // 2
KH = HEAD_DIM // 2
KQ = KH // 2
VQ = HEAD_DIM // 4


def _rwkv_col_orders(n_heads):
    assert 4 * n_heads == LANES
    c = np.arange(n_heads * HEAD_DIM)
    h, rest = c % n_heads, c // n_heads
    pk = h * HEAD_DIM + ((rest // 2) % 2) * KH + (rest % 2) * KQ + rest // 4
    pv = h * HEAD_DIM + (rest % 4) * VQ + rest // 4
    return pk, pv


def _head_select_minor(n_cols, n_heads):
    c = lax.broadcasted_iota(jnp.int32, (n_cols, n_heads), 0)
    h = lax.broadcasted_iota(jnp.int32, (n_cols, n_heads), 1)
    return jnp.where(_imod(c, n_heads) == h, 1.0, 0.0).astype(BF16)


def _head_expand_minor(n_heads, n_cols):
    h = lax.broadcasted_iota(jnp.int32, (n_heads, n_cols), 0)
    c = lax.broadcasted_iota(jnp.int32, (n_heads, n_cols), 1)
    return jnp.where(_imod(c, n_heads) == h, 1.0, 0.0).astype(BF16)


def _pair_to_chain_k(x0, x1):
    tm = x0.shape[0]
    a = x0.reshape(tm, KQ, LANES)
    b = x1.reshape(tm, KQ, LANES)
    lane = lax.broadcasted_iota(jnp.int32, (tm, KQ, LANES), 2)
    seq0 = _imod(lane, CHAINS) < CHAINS // 2
    t0 = jnp.where(seq0, a, pltpu.roll(b, CHAINS // 2, 2))
    t1 = jnp.where(seq0, pltpu.roll(a, LANES - CHAINS // 2, 2), b)
    return jnp.concatenate([t0, t1], axis=1)


def _pair_to_chain_v(x0, x1):
    tm = x0.shape[0]
    a = x0.reshape(tm, VQ, LANES)
    b = x1.reshape(tm, VQ, LANES)
    q = CHAINS // 2
    grp = _idiv(lax.broadcasted_iota(jnp.int32, (tm, VQ, LANES), 2), q)
    outs = []
    for j in range(4):
        pa = jnp.where(grp == j, a, 0.0)
        pb = jnp.where(grp == j, b, 0.0)
        sa, sb = (LANES - q * j) % LANES, (q - q * j) % LANES
        y = (pltpu.roll(pa, sa, 2) if sa else pa) + (pltpu.roll(pb, sb, 2) if sb else pb)
        outs.append(y + pltpu.roll(y, CHAINS, 2))
    return jnp.concatenate(outs, axis=1)


def _chain_to_pair_v(o):
    tm = o.shape[0]
    q = CHAINS // 2
    grp = _idiv(lax.broadcasted_iota(jnp.int32, (tm, VQ, LANES), 2), q)
    res = []
    for s in range(2):
        acc = jnp.zeros((tm, VQ, LANES), F32)
        for j in range(4):
            piece = o[:, j * VQ:(j + 1) * VQ, :]
            sh = (q * j - q * s) % LANES
            acc = jnp.where(grp == j, pltpu.roll(piece, sh, 2) if sh else piece, acc)
        res.append(acc.reshape(tm, VQ * LANES))
    return res


def _rwkv_prep_kernel(*refs, pad, tm, dr, n_heads, sample):
    if sample:
        (rkv_ref, lora_ref, sp_rkv_ref, sp_lora_ref, mu_rkv_ref, mu_lora_ref, wd_ref, wa_ref, wg_ref,
         w0_ref, a0_ref, kk_s_ref, ka_s_ref, rk_ref,
         r_o, w_o, k_o, v_o, kk_o, b_o, g_o, bon_o) = refs
    else:
        (rkv_ref, lora_ref, mu_rkv_ref, mu_lora_ref, wd_ref, wa_ref, wg_ref,
         w0_ref, a0_ref, kk_s_ref, ka_s_ref, rk_ref,
         r_o, w_o, k_o, v_o, kk_o, b_o, g_o, bon_o, carry_rkv, carry_lora) = refs
    i = pl.program_id(1)
    row = lax.broadcasted_iota(jnp.int32, (tm, 1), 0)
    sel = _head_select_minor(dr, n_heads)
    exp = _head_expand_minor(n_heads, dr)

    if not sample:
        @pl.when(i == 0)
        def _():
            carry_rkv[...] = jnp.zeros_like(carry_rkv)
            carry_lora[...] = jnp.zeros_like(carry_lora)

    per_seq = []
    for s in range(2):
        p_rkv = rkv_ref[s]
        p_lora = lora_ref[s]
        if sample:
            prev_rkv = jnp.where(row == 0, sp_rkv_ref[s], pltpu.roll(p_rkv, 1, 0))
            prev_lora = jnp.where(row == 0, sp_lora_ref[s], pltpu.roll(p_lora, 1, 0))
        else:
            pos = i * tm + row
            real = pos >= pad
            p_rkv = jnp.where(real, p_rkv, 0.0)
            p_lora = jnp.where(real, p_lora, 0.0)
            prev_rkv = jnp.where(row == 0, carry_rkv[s, 0:1, :], pltpu.roll(p_rkv, 1, 0))
            prev_lora = jnp.where(row == 0, carry_lora[s, 0:1, :], pltpu.roll(p_lora, 1, 0))
            prev_rkv = jnp.where(pos > pad, prev_rkv, 0.0)
            prev_lora = jnp.where(pos > pad, prev_lora, 0.0)
            carry_rkv[s, 0:1, :] = p_rkv[tm - 1:tm, :]
            carry_lora[s, 0:1, :] = p_lora[tm - 1:tm, :]

        xs_rkv = p_rkv + (prev_rkv - p_rkv) * mu_rkv_ref[...]
        xl = p_lora + (prev_lora - p_lora) * mu_lora_ref[...]
        r = xs_rkv[:, 0:dr]
        k = xs_rkv[:, dr:2 * dr]
        v = xs_rkv[:, 2 * dr:3 * dr]

        dec = _dot(jnp.tanh(xl).astype(BF16), wd_ref[...])
        a_pre = _dot(xl.astype(BF16), wa_ref[...])
        g = _dot(_sigmoid(xl).astype(BF16), wg_ref[...])
        w_log = -_softplus(-(w0_ref[...] + dec)) - 0.5
        decay = jnp.exp(-jnp.exp(w_log))
        a = _sigmoid(a0_ref[...] + a_pre)

        kk = k * kk_s_ref[...]
        nrm = jnp.sqrt(_head_sum_bcast(kk * kk, sel, exp))
        kk = kk / jnp.maximum(nrm, 1e-12)
        k_mod = k * (1.0 + (a - 1.0) * ka_s_ref[...])
        g_o[s] = g
        bon_o[s] = _head_sum_bcast(r * k_mod * rk_ref[...], sel, exp) * v
        per_seq.append((r, decay, k_mod, kk, kk * a, v))

    for idx, out in enumerate((r_o, w_o, k_o, kk_o, b_o)):
        out[...] = _pair_to_chain_k(per_seq[0][idx], per_seq[1][idx])
    v_o[...] = _pair_to_chain_v(per_seq[0][5], per_seq[1][5])


def _rwkv_prep(p3, sp_rkv, sp_lora, prm, *, pad, sample):
    n_seq, t, _ = p3.shape
    assert n_seq % 2 == 0
    n_pairs = n_seq // 2
    dr = prm["dr"]
    dl = prm["lora_w"]
    tm = t if sample else _pick(t, (64, 32, 16, 8))
    lora_blk = (3 * dr + 3 * prm["df"]) // dl
    full = lambda a: pl.BlockSpec(a.shape, lambda p, i: (0,) * a.ndim)
    consts = [prm["mu_rkv"], prm["mu_lora"], prm["wd"], prm["wa"], prm["wg"], prm["w0"], prm["a0"],
              prm["k_k"], prm["k_a"], prm["r_k"]]
    in_specs = [pl.BlockSpec((2, tm, 3 * dr), lambda p, i: (p, i, 0)),
                pl.BlockSpec((2, tm, dl), lambda p, i: (p, i, lora_blk))]
    args = [p3, p3]
    if sample:
        in_specs += [pl.BlockSpec((2, tm, 3 * dr), lambda p, i: (p, i, 0)),
                     pl.BlockSpec((2, tm, dl), lambda p, i: (p, i, 0))]
        args += [sp_rkv, sp_lora]
    in_specs += [full(c) for c in consts]
    args += consts
    kspec = pl.BlockSpec((tm, KH, LANES), lambda p, i: (i, 0, p))
    vspec = pl.BlockSpec((tm, HEAD_DIM, LANES), lambda p, i: (i, 0, p))
    rspec = pl.BlockSpec((2, tm, dr), lambda p, i: (p, i, 0))
    kshape = jax.ShapeDtypeStruct((t, KH, n_pairs * LANES), F32)
    vshape = jax.ShapeDtypeStruct((t, HEAD_DIM, n_pairs * LANES), F32)
    rshape = jax.ShapeDtypeStruct((n_seq, t, dr), F32)
    scratch = [] if sample else [pltpu.VMEM((2, 8, 3 * dr), F32), pltpu.VMEM((2, 8, dl), F32)]
    return pl.pallas_call(
        functools.partial(_rwkv_prep_kernel, pad=pad, tm=tm, dr=dr, n_heads=dr // HEAD_DIM, sample=sample),
        grid=(n_pairs, t // tm),
        in_specs=in_specs,
        out_specs=[kspec, kspec, kspec, vspec, kspec, kspec, rspec, rspec],
        out_shape=[kshape, kshape, kshape, vshape, kshape, kshape, rshape, rshape],
        scratch_shapes=scratch,
        compiler_params=_cparams(("parallel", "arbitrary")),
        name="rwkv_prep_sample" if sample else "rwkv_prep_prompt",
    )(*args)


def _rwkv_scan_kernel(r_ref, w_ref, k_ref, kk_ref, b_ref, v_ref, s0_ref, o_ref, sf_ref, s_scr, *, tb):
    t_blk = pl.program_id(1)

    @pl.when(t_blk == 0)
    def _():
        s_scr[...] = s0_ref[...]

    def step(t, carry):
        r = r_ref[t]
        w = w_ref[t]
        k = k_ref[t]
        kk = kk_ref[t]
        b = b_ref[t]
        for vi in range(HEAD_DIM):
            sv = s_scr[vi]
            skk = jnp.sum(sv * kk, axis=0, keepdims=True)
            skk = skk + pltpu.roll(skk, CHAINS, 1)
            vrow = v_ref[t, pl.ds(vi, 1), :]
            sv = sv * w - skk * b + vrow * k
            s_scr[vi] = sv
            o = jnp.sum(sv * r, axis=0, keepdims=True)
            o_ref[t, pl.ds(vi, 1), :] = o + pltpu.roll(o, CHAINS, 1)
        return carry

    lax.fori_loop(0, tb, step, 0)

    @pl.when(t_blk == pl.num_programs(1) - 1)
    def _():
        sf_ref[...] = s_scr[...]


def _state_to_chain_layout(s):
    n_seq, h = s.shape[:2]
    g = (n_seq * h) // CHAINS
    s = s.reshape(g, CHAINS, HEAD_DIM, 2, KH).transpose(2, 4, 0, 3, 1)
    return s.reshape(HEAD_DIM, KH, g * LANES)


def _state_from_chain_layout(s, n_seq, h):
    g = (n_seq * h) // CHAINS
    s = s.reshape(HEAD_DIM, KH, g, 2, CHAINS).transpose(2, 4, 0, 3, 1)
    return s.reshape(n_seq, h, HEAD_DIM, HEAD_DIM)


def _rwkv_scan(r, w, k, kk, b, v, s0):
    n_seq, h = s0.shape[:2]
    t = r.shape[0]
    g = (n_seq * h) // CHAINS
    tb = _pick(t, (64, 32, 16, 8, 4, 2, 1))
    ins = [r, w, k, kk, b]
    vin = v
    sin = _state_to_chain_layout(s0)
    kspec = pl.BlockSpec((tb, KH, LANES), lambda gi, ti: (ti, 0, gi))
    vspec = pl.BlockSpec((tb, HEAD_DIM, LANES), lambda gi, ti: (ti, 0, gi))
    sspec = pl.BlockSpec((HEAD_DIM, KH, LANES), lambda gi, ti: (0, 0, gi))
    o, sf = pl.pallas_call(
        functools.partial(_rwkv_scan_kernel, tb=tb),
        grid=(g, t // tb),
        in_specs=[kspec] * 5 + [vspec, sspec],
        out_specs=[vspec, sspec],
        out_shape=[jax.ShapeDtypeStruct((t, HEAD_DIM, g * LANES), F32),
                   jax.ShapeDtypeStruct((HEAD_DIM, KH, g * LANES), F32)],
        scratch_shapes=[pltpu.VMEM((HEAD_DIM, KH, LANES), F32)],
        compiler_params=_cparams(("parallel", "arbitrary")),
        name="rwkv_scan",
    )(*ins, vin, sin)
    return o, _state_from_chain_layout(sf, n_seq, h)


def _rwkv_post_kernel(o_ref, g_ref, bon_ref, gg_ref, gb_ref, out_ref, *, dr):
    n_heads = dr // HEAD_DIM
    sel = _head_select_minor(dr, n_heads)
    exp = _head_expand_minor(n_heads, dr)
    for s, o in enumerate(_chain_to_pair_v(o_ref[...])):
        mu = _head_sum_bcast(o, sel, exp) * (1.0 / HEAD_DIM)
        oc = o - mu
        var = _head_sum_bcast(oc * oc, sel, exp) * (1.0 / HEAD_DIM)
        y = oc * lax.rsqrt(var + GN_EPS) * gg_ref[...] + gb_ref[...]
        out_ref[s] = ((y + bon_ref[s]) * g_ref[s]).astype(BF16)


def _rwkv_post(o, g, bonus, gn_g, gn_b):
    n_seq, t, dr = g.shape
    tm = t if t <= 64 else _pick(t, (64, 32, 16, 8))
    row = pl.BlockSpec((2, tm, dr), lambda p, i: (p, i, 0))
    vec = pl.BlockSpec((1, dr), lambda p, i: (0, 0))
    return pl.pallas_call(
        functools.partial(_rwkv_post_kernel, dr=dr),
        grid=(n_seq // 2, t // tm),
        in_specs=[pl.BlockSpec((tm, HEAD_DIM, LANES), lambda p, i: (i, 0, p)), row, row, vec, vec],
        out_specs=row,
        out_shape=jax.ShapeDtypeStruct((n_seq, t, dr), BF16),
        compiler_params=_cparams(("parallel", "parallel")),
        name="rwkv_post",
    )(o, g, bonus, gn_g.reshape(1, dr), gn_b.reshape(1, dr))


def _logf_kernel(lora_ref, bf_ref, lf_ref, c_ref, carry, *, tb, pad, f_off, n_heads):
    i = pl.program_id(1)

    @pl.when(i == 0)
    def _():
        carry[...] = jnp.zeros_like(carry)

    z = lora_ref[:, f_off:f_off + n_heads] + bf_ref[...]
    lf = -_softplus(-z)
    pos = i * tb + lax.broadcasted_iota(jnp.int32, (tb, 1), 0)
    lf = jnp.where(pos >= pad, lf, 0.0)
    rr = lax.broadcasted_iota(jnp.int32, (tb, tb), 0)
    cc = lax.broadcasted_iota(jnp.int32, (tb, tb), 1)
    tril = jnp.where(cc <= rr, 1.0, 0.0).astype(BF16)
    c = _dot_01x(tril, lf) + carry[0:1, :]
    lf_ref[...] = lf
    c_ref[...] = c
    carry[0:1, :] = c[tb - 1:tb, :]


def _logf_cumsum(p, b_f, prm, n_seq, seq_rows, pad):
    dl = prm["lora_w"]
    hf = b_f.shape[-1]
    tb = _pick(seq_rows, (128, 64, 32, 16, 8))
    lora_blk = (3 * prm["dr"] + 3 * prm["df"]) // dl
    p3 = p.reshape(n_seq, seq_rows, p.shape[-1])
    out = pl.BlockSpec((None, tb, hf), lambda s, i: (s, i, 0))
    return pl.pallas_call(
        functools.partial(_logf_kernel, tb=tb, pad=pad, f_off=prm["f_off"], n_heads=hf),
        grid=(n_seq, seq_rows // tb),
        in_specs=[pl.BlockSpec((None, tb, dl), lambda s, i: (s, i, lora_blk)),
                  pl.BlockSpec((1, hf), lambda s, i: (0, 0))],
        out_specs=[out, out],
        out_shape=[jax.ShapeDtypeStruct((n_seq, seq_rows, hf), F32)] * 2,
        scratch_shapes=[pltpu.VMEM((8, hf), F32)],
        compiler_params=_cparams(("parallel", "arbitrary")),
        name="logf_cumsum",
    )(p3, b_f.reshape(1, hf))


LOG2E = 1.4426950408889634


def _bf16_terms(x):
    hi = x.astype(BF16)
    r = x - hi.astype(F32)
    mid = r.astype(BF16)
    lo = (r - mid.astype(F32)).astype(BF16)
    return hi.astype(F32), mid.astype(F32), lo.astype(F32)


def _augment(x, own, lane, first, ones_at, terms):
    t0, t1, t2 = terms
    a = first
    extra = jnp.where(lane == a + (3 - ones_at), t0, jnp.where(lane == a + (4 - ones_at), t1,
            jnp.where(lane == a + (5 - ones_at), t2, 0.0)))
    extra = jnp.where((lane >= a + ones_at) & (lane < a + ones_at + 3), 1.0, extra)
    return jnp.where(own, x, extra).astype(BF16)


def _fox_prompt_kernel(q_ref, k_ref, v_ref, cq_ref, ck_ref, o_ref, ka_scr, va_scr, m_scr, acc_scr, *, tq, scale2):
    i = pl.program_id(2)
    seq_rows = k_ref.shape[0]

    @pl.when(i == 0)
    def _():
        lane_k = lax.broadcasted_iota(jnp.int32, (seq_rows, LANES), 1)
        k = k_ref[...]
        v = v_ref[...]
        for h in range(2):
            own = (lane_k < HEAD_DIM) if h == 0 else (lane_k >= HEAD_DIM)
            first = HEAD_DIM * (1 - h)
            ka_scr[h] = _augment(k, own, lane_k, first, 0, _bf16_terms(-ck_ref[:, h:h + 1]))
            va_scr[h] = jnp.where(own, v, 1.0).astype(BF16)

    lane = lax.broadcasted_iota(jnp.int32, (tq, LANES), 1)
    q = q_ref[...] * scale2
    qa = []
    for h in range(2):
        own = (lane < HEAD_DIM) if h == 0 else (lane >= HEAD_DIM)
        qa.append(_augment(q, own, lane, HEAD_DIM * (1 - h), 3, _bf16_terms(cq_ref[:, h:h + 1])))
    m_scr[...] = jnp.full_like(m_scr, NEG_INF)
    acc_scr[...] = jnp.zeros_like(acc_scr)

    def block(j, width, diagonal):
        off = pl.multiple_of(j * tq, tq)
        for h in range(2):
            s = _dot_nt(qa[h], ka_scr[h, pl.ds(off, width), :])
            if diagonal:
                rr = lax.broadcasted_iota(jnp.int32, (tq, width), 0)
                cc = lax.broadcasted_iota(jnp.int32, (tq, width), 1)
                s = jnp.where(cc <= rr, s, NEG_INF)
            cols = [s[:, c * LANES:(c + 1) * LANES] for c in range(width // LANES)]
            m_blk = jnp.max(functools.reduce(jnp.maximum, cols), axis=-1, keepdims=True)
            m_prev = m_scr[h]
            m_new = jnp.maximum(m_prev, m_blk)
            p = jnp.concatenate([jnp.exp2(c - m_new) for c in cols], axis=1).astype(BF16)
            acc_scr[h] = jnp.exp2(m_prev - m_new) * acc_scr[h] + _dot(p, va_scr[h, pl.ds(off, width), :])
            m_scr[h] = m_new

    def body(jj, carry):
        block(2 * jj, 2 * tq, False)
        return carry

    lax.fori_loop(0, i // 2, body, 0)

    @pl.when(i % 2 == 1)
    def _():
        block(i - 1, tq, False)

    block(i, tq, True)
    a0 = acc_scr[0]
    a1 = acc_scr[1]
    o = jnp.where(lane < HEAD_DIM, a0 / pltpu.roll(a0, HEAD_DIM, 1), a1 / pltpu.roll(a1, HEAD_DIM, 1))
    o_ref[...] = o.astype(BF16)


def _fox_prompt(p, c, prm, n_seq, seq_rows, pad):
    dr, df = prm["dr"], prm["df"]
    hf = df // HEAD_DIM
    n_pairs = hf // 2
    tq = _pick(seq_rows, (384, 256, 128))
    nq = seq_rows // tq
    p3 = p.reshape(n_seq, seq_rows, p.shape[-1])
    c4 = (c * LOG2E).reshape(n_seq, seq_rows, n_pairs, 2)
    cq = c4.transpose(0, 2, 1, 3)
    is_pad = (jnp.arange(seq_rows) < pad)[None, :, None, None]
    ck = jnp.where(is_pad, -NEG_INF, c4).transpose(0, 2, 1, 3)
    q0, k0, v0 = (3 * dr) // LANES, (3 * dr + df) // LANES, (3 * dr + 2 * df) // LANES
    return pl.pallas_call(
        functools.partial(_fox_prompt_kernel, tq=tq, scale2=HEAD_DIM ** -0.5 * LOG2E),
        grid=(n_seq, n_pairs, nq),
        in_specs=[pl.BlockSpec((None, tq, LANES), lambda b, pr, i: (b, i, q0 + pr)),
                  pl.BlockSpec((None, seq_rows, LANES), lambda b, pr, i: (b, 0, k0 + pr)),
                  pl.BlockSpec((None, seq_rows, LANES), lambda b, pr, i: (b, 0, v0 + pr)),
                  pl.BlockSpec((None, None, tq, 2), lambda b, pr, i: (b, pr, i, 0)),
                  pl.BlockSpec((None, None, seq_rows, 2), lambda b, pr, i: (b, pr, 0, 0))],
        out_specs=pl.BlockSpec((None, tq, LANES), lambda b, pr, i: (b, i, pr)),
        out_shape=jax.ShapeDtypeStruct((n_seq, seq_rows, df), BF16),
        scratch_shapes=[pltpu.VMEM((2, seq_rows, LANES), BF16), pltpu.VMEM((2, seq_rows, LANES), BF16),
                        pltpu.VMEM((2, tq, LANES), F32), pltpu.VMEM((2, tq, LANES), F32)],
        compiler_params=_cparams(("parallel", "parallel", "arbitrary")),
        name="fox_prompt",
    )(p3, p3, p3, cq, ck).reshape(n_seq * seq_rows, df)


def _fox_sample_kernel(pt_ref, q_ref, kn_ref, vn_ref, cn_ref, cnrow_ref, *rest, pp, page, nq, hf, scale):
    lf_refs = rest[:pp]
    k_refs = rest[pp:2 * pp]
    v_refs = rest[2 * pp:3 * pp]
    o_ref, qbd_scr, m_scr, l_scr, acc_scr, carry_scr = rest[3 * pp:]
    step = pl.program_id(1)
    gh = MXU_DEPTH // HEAD_DIM
    ng = hf // gh
    gr = gh * nq
    gd = gh * HEAD_DIM
    rows = hf * nq
    own = (_idiv(lax.broadcasted_iota(jnp.int32, (gr, gd), 0), nq)
           == _idiv(lax.broadcasted_iota(jnp.int32, (gr, gd), 1), HEAD_DIM))

    @pl.when(step == 0)
    def _():
        q = q_ref[...] * scale
        for g in range(ng):
            qg = jnp.broadcast_to(q[None, :, g * gd:(g + 1) * gd], (gh, nq, gd)).reshape(gr, gd)
            qbd_scr[g] = jnp.where(own, qg, 0.0).astype(BF16)
        m_scr[...] = jnp.full_like(m_scr, NEG_INF)
        l_scr[...] = jnp.zeros_like(l_scr)
        acc_scr[...] = jnp.zeros_like(acc_scr)
        carry_scr[...] = jnp.zeros_like(carry_scr)

    tr = lax.broadcasted_iota(jnp.int32, (page, 2 * page), 0)
    tc = lax.broadcasted_iota(jnp.int32, (page, 2 * page), 1)
    suffix = jnp.where((tc >= page) | (tr > tc), 1.0, 0.0).astype(BF16)
    carry = carry_scr[...]
    later_sum = [None] * pp
    for x in reversed(range(pp)):
        r = _dot_x01(lf_refs[x][...], suffix)
        later_sum[x] = r[:, :page] + carry
        carry = carry + r[:, page:]
    carry_scr[...] = carry
    bias = jnp.concatenate(later_sum, axis=1)
    bias = jnp.broadcast_to(bias[:, None, :], (hf, nq, pp * page)).reshape(rows, pp * page)

    group = lambda ref, g: ref[g * gd:(g + 1) * gd, :].astype(BF16)
    s = jnp.concatenate(
        [jnp.concatenate([_dot(qbd_scr[g], group(k_refs[x], g)) for g in range(ng)], axis=0) for x in range(pp)],
        axis=1)
    s = s + cn_ref[...] + bias
    m_prev = m_scr[...]
    m_new = jnp.maximum(m_prev, jnp.max(s, axis=-1, keepdims=True))
    alpha = jnp.exp(m_prev - m_new)
    p = jnp.exp(s - m_new)
    l_scr[...] = alpha * l_scr[...] + jnp.sum(p, axis=-1, keepdims=True)
    m_scr[...] = m_new
    for g in range(ng):
        rws = slice(g * gr, (g + 1) * gr)
        acc = alpha[rws] * acc_scr[g]
        for x in range(pp):
            acc = acc + _dot_nt(p[rws, x * page:(x + 1) * page].astype(BF16), group(v_refs[x], g))
        acc_scr[g] = acc

    @pl.when(step == pl.num_programs(1) - 1)
    def _():
        q = (q_ref[...] * scale).astype(BF16)
        kn = kn_ref[...].astype(BF16)
        vn = vn_ref[...].astype(BF16)
        head = lambda a, h: a[:, h * HEAD_DIM:(h + 1) * HEAD_DIM]
        sn = jnp.stack([_dot_nt(head(q, h), head(kn, h)) for h in range(hf)])
        qi = lax.broadcasted_iota(jnp.int32, (hf, nq, nq), 1)
        ki = lax.broadcasted_iota(jnp.int32, (hf, nq, nq), 2)
        sn = jnp.where(ki <= qi, sn + cnrow_ref[...], NEG_INF).reshape(rows, nq)
        m_last = m_scr[...]
        m_fin = jnp.maximum(m_last, jnp.max(sn, axis=-1, keepdims=True))
        a_fin = jnp.exp(m_last - m_fin)
        pn = jnp.exp(sn - m_fin)
        inv = 1.0 / (a_fin * l_scr[...] + jnp.sum(pn, axis=-1, keepdims=True))
        pn = (pn * inv).astype(BF16)
        o_new = jnp.concatenate([_dot(pn[h * nq:(h + 1) * nq, :], head(vn, h)) for h in range(hf)], axis=1)
        w_past = a_fin * inv
        o_past = jnp.concatenate(
            [jnp.sum(jnp.where(own, acc_scr[g] * w_past[g * gr:(g + 1) * gr], 0.0).reshape(gh, nq, gd), axis=0)
             for g in range(ng)], axis=1)
        o_ref[...] = (o_past + o_new).astype(BF16)


def _fox_sample(p_s, c_new, cache_k, cache_v, cache_lft, page_table_flat, prm, n_seq, nq, n_pages, layer):
    dr, df = prm["dr"], prm["df"]
    hf = df // HEAD_DIM
    page = cache_k.shape[3]
    gh = MXU_DEPTH // HEAD_DIM
    assert hf % gh == 0
    pp = _pick(n_pages, (4, 2, 1))
    npb = n_pages // pp
    rows = hf * nq
    p3 = p_s.reshape(n_seq, nq, p_s.shape[-1])
    cn_t = c_new.transpose(0, 2, 1)
    cn = cn_t.reshape(n_seq, rows, 1)
    cn_row = cn_t[..., None] - cn_t[:, :, None, :]
    qb, kb, vb = (3 * dr) // df, (3 * dr + df) // df, (3 * dr + 2 * df) // df
    seq_blk = lambda blk: pl.BlockSpec((None, nq, df), lambda b, i, pt: (b, 0, blk))

    def page_of(b, i, pt, x):
        return pt[b * n_pages + (npb - 1 - i) * pp + x]

    def kv_spec(x):
        return pl.BlockSpec((None, None, df, page), lambda b, i, pt: (layer, page_of(b, i, pt, x), 0, 0))

    def lf_spec(x):
        return pl.BlockSpec((None, None, hf, page), lambda b, i, pt: (layer, page_of(b, i, pt, x), 0, 0))

    grid_spec = pltpu.PrefetchScalarGridSpec(
        num_scalar_prefetch=1,
        grid=(n_seq, npb),
        in_specs=[seq_blk(qb), seq_blk(kb), seq_blk(vb),
                  pl.BlockSpec((None, rows, 1), lambda b, i, pt: (b, 0, 0)),
                  pl.BlockSpec((None, hf, nq, nq), lambda b, i, pt: (b, 0, 0, 0))]
                 + [lf_spec(x) for x in range(pp)] + [kv_spec(x) for x in range(pp)] * 2,
        out_specs=pl.BlockSpec((None, nq, df), lambda b, i, pt: (b, 0, 0)),
        scratch_shapes=[pltpu.VMEM((hf // gh, gh * nq, MXU_DEPTH), BF16), pltpu.VMEM((rows, 1), F32),
                        pltpu.VMEM((rows, 1), F32), pltpu.VMEM((hf // gh, gh * nq, MXU_DEPTH), F32),
                        pltpu.VMEM((hf, page), F32)],
    )
    out = pl.pallas_call(
        functools.partial(_fox_sample_kernel, pp=pp, page=page, nq=nq, hf=hf, scale=HEAD_DIM ** -0.5),
        grid_spec=grid_spec,
        out_shape=jax.ShapeDtypeStruct((n_seq, nq, df), BF16),
        compiler_params=_cparams(("parallel", "arbitrary")),
        name="fox_sample",
    )(page_table_flat, p3, p3, p3, cn, cn_row, *([cache_lft] * pp), *([cache_k] * pp), *([cache_v] * pp))
    return out.reshape(n_seq * nq, df)


def _gelu_tanh(x):
    return 0.5 * x * (1.0 + jnp.tanh(0.7978845608028654 * (x + 0.044715 * (x * x * x))))


def _ffn_in_kernel(*refs, tm, tn, seq_rows, pad, sample):
    if sample:
        x_ref, wu_ref, wg_ref, cw_ref, cb_ref, p1_ref, p2_ref, h_ref, u_ref = refs
    else:
        x_ref, wu_ref, wg_ref, cw_ref, cb_ref, h_ref, tail_ref, carry = refs
    i = pl.program_id(0)
    j = pl.program_id(1)
    x = x_ref[...]
    u = _dot(x, wu_ref[...].astype(BF16))
    gate = _dot(x, wg_ref[...].astype(BF16))
    row = lax.broadcasted_iota(jnp.int32, (tm, 1), 0)
    if sample:
        t = _imod(row, seq_rows)
        u1 = jnp.where(t == 0, p1_ref[...], pltpu.roll(u, 1, 0))
        u2 = jnp.where(t < 2, p2_ref[...], pltpu.roll(u, 2, 0))
        u_ref[...] = u
    else:
        @pl.when(i == 0)
        def _():
            carry[:, pl.ds(pl.multiple_of(j * tn, tn), tn)] = jnp.zeros((8, tn), F32)

        pos = (i * tm) % seq_rows + row
        u = jnp.where(pos >= pad, u, 0.0)
        prev = carry[:, pl.ds(pl.multiple_of(j * tn, tn), tn)]
        u1 = jnp.where(row == 0, prev[7:8, :], pltpu.roll(u, 1, 0))
        u2 = jnp.where(row == 0, prev[6:7, :], jnp.where(row == 1, prev[7:8, :], pltpu.roll(u, 2, 0)))
        tail = u[tm - 8:tm, :]
        carry[:, pl.ds(pl.multiple_of(j * tn, tn), tn)] = tail
        tail_ref[...] = tail
    c = cb_ref[...] + cw_ref[0:1, :] * u2 + cw_ref[1:2, :] * u1 + cw_ref[2:3, :] * u
    h_ref[...] = (_gelu_tanh(c) * gate).astype(BF16)


def _ffn_in(x1b, w_ffn_in, conv_w, conv_b, p1, p2, *, seq_rows, pad, sample):
    m, d = x1b.shape
    dff = conv_w.shape[-1]
    tn = _pick(dff, (256, 128))
    nj = dff // tn
    tm = m if sample else _pick(seq_rows, (1056, 1024, 512, 384, 256, 128, 64, 32, 16, 8))
    ni = m // tm
    in_specs = [pl.BlockSpec((tm, d), lambda i, j: (i, 0)),
                pl.BlockSpec((d, tn), lambda i, j: (0, j)),
                pl.BlockSpec((d, tn), lambda i, j: (0, j + nj)),
                pl.BlockSpec((3, tn), lambda i, j: (0, j)),
                pl.BlockSpec((1, tn), lambda i, j: (0, j))]
    args = [x1b, w_ffn_in, w_ffn_in, conv_w, conv_b.reshape(1, dff)]
    tile = pl.BlockSpec((tm, tn), lambda i, j: (i, j))
    if sample:
        in_specs += [tile, tile]
        args += [p1, p2]
        out_specs = [tile, tile]
        out_shape = [jax.ShapeDtypeStruct((m, dff), BF16), jax.ShapeDtypeStruct((m, dff), F32)]
        scratch = []
    else:
        out_specs = [tile, pl.BlockSpec((8, tn), lambda i, j: (i, j))]
        out_shape = [jax.ShapeDtypeStruct((m, dff), BF16), jax.ShapeDtypeStruct((ni * 8, dff), F32)]
        scratch = [pltpu.VMEM((8, dff), F32)]
    return pl.pallas_call(
        functools.partial(_ffn_in_kernel, tm=tm, tn=tn, seq_rows=seq_rows, pad=pad, sample=sample),
        grid=(ni, nj),
        in_specs=in_specs,
        out_specs=out_specs,
        out_shape=out_shape,
        scratch_shapes=scratch,
        compiler_params=_cparams(("arbitrary", "arbitrary")),
        name="ffn_in_sample" if sample else "ffn_in_prompt",
    )(*args)


def _layer_params(lw, d_model):
    (w_in, b_f, mu_shift, w0, w_decay_up, a0, w_iclr_up, w_gate_up, k_k, k_a, r_k,
     gn_g, gn_b, w_out, ln1_g, ln1_b, w_ffn_in, conv_w, conv_b, w_ffn_out, ln2_g, ln2_b) = lw
    dr = w0.shape[-1]
    hf = b_f.shape[-1]
    df = hf * HEAD_DIM
    d_dec, d_icl, d_gate = w_decay_up.shape[0], w_iclr_up.shape[0], w_gate_up.shape[0]
    rw_cols = 3 * dr + d_dec + d_icl + d_gate
    n_lora = d_dec + d_icl + d_gate
    lora_w = -(-(n_lora + hf) // (4 * LANES)) * (4 * LANES)
    assert (3 * dr + 3 * df) % lora_w == 0 and dr % LANES == 0 and df % LANES == 0
    pk, pv = _rwkv_col_orders(dr // HEAD_DIM)
    on_k = lambda a: a[..., pk]
    on_v = lambda a: a[..., pv]
    fox0 = rw_cols
    w_perm = jnp.concatenate(
        [on_k(w_in[:, :dr]), on_k(w_in[:, dr:2 * dr]), on_v(w_in[:, 2 * dr:3 * dr]),
         w_in[:, fox0:fox0 + 3 * df], w_in[:, 3 * dr:rw_cols],
         w_in[:, fox0 + 3 * df:], jnp.zeros((d_model, lora_w - n_lora - hf), w_in.dtype)], axis=1).astype(BF16)

    def pad_rows(w, off):
        return jnp.zeros((lora_w, dr), F32).at[off:off + w.shape[0]].set(w).astype(BF16)

    w_out_perm = jnp.concatenate([w_out[:dr][pv], w_out[dr:]], axis=0).astype(BF16)
    prm = dict(
        dr=dr, df=df, lora_w=lora_w, f_off=n_lora, pk=pk, pv=pv,
        mu_rkv=jnp.concatenate([on_k(mu_shift[:dr]), on_k(mu_shift[dr:2 * dr]),
                                on_v(mu_shift[2 * dr:3 * dr])]).reshape(1, -1),
        mu_lora=jnp.zeros((1, lora_w), F32).at[0, :n_lora].set(mu_shift[3 * dr:]),
        wd=pad_rows(on_k(w_decay_up), 0), wa=pad_rows(on_k(w_iclr_up), d_dec),
        wg=pad_rows(on_v(w_gate_up), d_dec + d_icl),
        w0=on_k(w0).reshape(1, dr), a0=on_k(a0).reshape(1, dr), k_k=on_k(k_k).reshape(1, dr),
        k_a=on_k(k_a).reshape(1, dr), r_k=on_k(r_k.reshape(dr)).reshape(1, dr), w_perm=w_perm,
        w_out=w_out_perm, w_ffn_in=w_ffn_in, w_ffn_out=w_ffn_out.astype(BF16), b_f=b_f,
        gn_g=on_v(gn_g), gn_b=on_v(gn_b), ln1=(ln1_g, ln1_b), ln2=(ln2_g, ln2_b), conv_w=conv_w, conv_b=conv_b)
    return prm


def _mm_tiles(m, n, seq_rows):
    tm = _pick(seq_rows, (1056, 1024, 768, 512, 384, 256, 128, 64, 32, 16, 8)) if m > 512 else m
    tn = _pick(n, (1024, 512, 256, 128))
    return tm, tn


def _permute_rw_row(row, prm):
    dr = prm["dr"]
    n_lora = prm["f_off"]
    lora = jnp.zeros(row.shape[:-1] + (prm["lora_w"],), F32).at[..., :n_lora].set(row[..., 3 * dr:])
    rkv = jnp.concatenate([row[..., :dr][..., prm["pk"]], row[..., dr:2 * dr][..., prm["pk"]],
                           row[..., 2 * dr:3 * dr][..., prm["pv"]]], axis=-1)
    return rkv, lora


def _unpermute_rkv(rkv, prm):
    dr = prm["dr"]
    ik, iv = np.argsort(prm["pk"]), np.argsort(prm["pv"])
    return jnp.concatenate([rkv[..., :dr][..., ik], rkv[..., dr:2 * dr][..., ik],
                            rkv[..., 2 * dr:3 * dr][..., iv]], axis=-1)


def _run_group(x2, ln0, prm, alpha, *, n_seq, seq_rows, pad, skip, sample, shift_prev, s0, ffn_prev, attend):
    ln0_g, ln0_b = ln0
    m, d = x2.shape
    dr, df = prm["dr"], prm["df"]
    h_b = _ln0(x2, ln0_g, ln0_b)
    tm, tn = _mm_tiles(m, prm["w_perm"].shape[1], seq_rows)
    p = _matmul(h_b, prm["w_perm"], tm, tn, "in_proj")

    p3 = p.reshape(n_seq, seq_rows, -1)
    if sample:
        sp_rkv, sp_lora = _permute_rw_row(shift_prev, prm)
        expand = lambda a: jnp.zeros((n_seq, seq_rows, a.shape[-1]), F32).at[:, 0].set(a)
        sp_rkv, sp_lora = expand(sp_rkv), expand(sp_lora)
    else:
        sp_rkv = sp_lora = None
    r, w, k, v, kk, b, g, bonus = _rwkv_prep(p3, sp_rkv, sp_lora, prm, pad=pad, sample=sample)
    o_scan, s_last = _rwkv_scan(r, w, k, kk, b, v, s0)
    o_rw = _rwkv_post(o_scan, g, bonus, prm["gn_g"], prm["gn_b"]).reshape(m, dr)

    lf, c = _logf_cumsum(p, prm["b_f"], prm, n_seq, seq_rows, pad)
    o_fox = attend(p, c)

    tm, tn = _mm_tiles(m, d, seq_rows)
    mix = _matmul2(o_rw, o_fox, prm["w_out"], tm, tn, "out_proj")
    x1, x1b = _ln1(x2, mix, ln0_g, ln0_b, *prm["ln1"], alpha)

    if sample:
        p1 = jnp.zeros((n_seq, seq_rows, ffn_prev.shape[-1]), F32).at[:, 0].set(ffn_prev[:, 1])
        p2 = p1.at[:, 0].set(ffn_prev[:, 0]).at[:, 1].set(ffn_prev[:, 1])
        p1, p2 = p1.reshape(m, -1), p2.reshape(m, -1)
    else:
        p1 = p2 = None
    hid, u_aux = _ffn_in(x1b, prm["w_ffn_in"], prm["conv_w"], prm["conv_b"], p1, p2,
                         seq_rows=seq_rows, pad=pad, sample=sample)
    dff = hid.shape[1]
    tm_o = _pick(seq_rows, (528, 512, 384, 256, 128, 64, 32, 16, 8)) if m > 512 else m
    ffn = _matmul(hid, prm["w_ffn_out"], tm_o, _pick(d, (512, 256, 128)), "ffn_out")
    y = _ln2(x1, ffn, *prm["ln2"], alpha, n_seq, seq_rows, skip)

    hf = df // HEAD_DIM
    k_new = p3[:, pad:, 3 * dr + df:3 * dr + 2 * df].reshape(n_seq, seq_rows - pad, hf, HEAD_DIM)
    v_new = p3[:, pad:, 3 * dr + 2 * df:3 * dr + 3 * df].reshape(n_seq, seq_rows - pad, hf, HEAD_DIM)
    last = p3[:, -1]
    shift_last = jnp.concatenate([_unpermute_rkv(last[:, :3 * dr], prm),
                                  last[:, 3 * dr + 3 * df:3 * dr + 3 * df + prm["f_off"]]], axis=1)
    if sample:
        conv_state = u_aux.reshape(n_seq, seq_rows, dff)[:, -2:]
    else:
        tiles_per_seq = u_aux.shape[0] // 8 // n_seq
        conv_state = u_aux.reshape(n_seq, tiles_per_seq, 8, dff)[:, -1, -2:]
    return y, (k_new, v_new, lf[:, pad:], s_last, shift_last, conv_state)


def kernel(x_prompt, x_sample, cache_k, cache_v, cache_logf, state_rwkv, state_shift, state_ffn_conv,
           page_table, meta_tokens, ln0_g, ln0_b, w_in, b_f, mu_shift, w0, w_decay_up, a0, w_iclr_up,
           w_gate_up, k_k, k_a, r_k, gn_g, gn_b, w_out, ln1_g, ln1_b, w_ffn_in, conv_w, conv_b,
           w_ffn_out, ln2_g, ln2_b):
    depth = w_in.shape[0]
    assert depth == 1, "the token layout between layers is only wired for a single layer"
    b, seq, d = x_prompt.shape
    db, nq, _ = x_sample.shape
    n_meta = meta_tokens.shape[0]
    alpha = (2 * depth) ** 0.25
    n_pool, page = cache_k.shape[1:3]
    n_pages = page_table.shape[1]
    h_rw = r_k.shape[1]

    t_real = seq + n_meta
    pad = (-t_real) % Q_BLOCK
    t_pad = t_real + pad
    meta = jnp.broadcast_to(meta_tokens[None].astype(x_prompt.dtype), (b, n_meta, d))
    xp = jnp.concatenate([jnp.zeros((b, pad, d), x_prompt.dtype), meta, x_prompt], axis=1).reshape(b * t_pad, d)
    xs = x_sample.reshape(db * nq, d)
    pt_flat = page_table.reshape(-1).astype(jnp.int32)

    outs_p, outs_s = [], []
    for l in range(depth):
        lw = tuple(wt[l] for wt in (w_in, b_f, mu_shift, w0, w_decay_up, a0, w_iclr_up, w_gate_up, k_k, k_a,
                                     r_k, gn_g, gn_b, w_out, ln1_g, ln1_b, w_ffn_in, conv_w, conv_b,
                                     w_ffn_out, ln2_g, ln2_b))
        prm = _layer_params(lw, d)
        hf = prm["df"] // HEAD_DIM

        attend_p = lambda p, c: _fox_prompt(p, c, prm, b, t_pad, pad)
        y_p, st_p = _run_group(xp, (ln0_g, ln0_b), prm, alpha, n_seq=b, seq_rows=t_pad, pad=pad, skip=pad + n_meta, sample=False,
                               shift_prev=None, s0=jnp.zeros((b, h_rw, HEAD_DIM, HEAD_DIM), F32),
                               ffn_prev=None, attend=attend_p)

        cache_kt = cache_k.transpose(0, 1, 3, 4, 2).reshape(depth, n_pool, hf * HEAD_DIM, page)
        cache_vt = cache_v.transpose(0, 1, 3, 4, 2).reshape(depth, n_pool, hf * HEAD_DIM, page)
        cache_lft = cache_logf.astype(F32).transpose(0, 1, 3, 2)
        attend_s = lambda p, c: _fox_sample(p, c, cache_kt, cache_vt, cache_lft, pt_flat, prm, db, nq, n_pages, l)
        y_s, st_s = _run_group(xs, (ln0_g, ln0_b), prm, alpha, n_seq=db, seq_rows=nq, pad=0, skip=0, sample=True,
                               shift_prev=state_shift[l], s0=state_rwkv[l], ffn_prev=state_ffn_conv[l],
                               attend=attend_s)
        outs_p.append(st_p)
        outs_s.append(st_s)

    k_p, v_p, lf_p, rw_p, sh_p, cv_p = (jnp.stack([o[i] for o in outs_p]) for i in range(6))
    k_s, v_s, lf_s, rw_s, sh_s, cv_s = (jnp.stack([o[i] for o in outs_s]) for i in range(6))
    y_prompt = y_p
    y_sample = y_s.reshape(db, nq, d)
    return (y_prompt, y_sample, k_p, v_p, lf_p, rw_p, sh_p, cv_p, k_s, v_s, lf_s, rw_s, sh_s, cv_s)
```

```python
import functools

import jax
import jax.numpy as jnp
import numpy as np
from jax import lax
from jax.experimental import pallas as pl
from jax.experimental.pallas import tpu as pltpu

F32 = jnp.float32
BF16 = jnp.bfloat16

HEAD_DIM = 64
LANES = 128
SUBLANES = 8
MXU_DEPTH = 256
Q_BLOCK = 128
LN_EPS = 1e-5
GN_EPS = 64e-5
NEG_INF = -1e30
VMEM_LIMIT = 56 * 1024 * 1024


def _cparams(sem):
    return pltpu.CompilerParams(dimension_semantics=sem, vmem_limit_bytes=VMEM_LIMIT)


def _pick(n, candidates):
    for c in candidates:
        if n % c == 0:
            return c
    return n


def _split_bf16(x):
    hi = x.astype(BF16)
    lo = (x - hi.astype(F32)).astype(BF16)
    return hi, lo


def _dot(a, b):
    return jnp.dot(a, b, preferred_element_type=F32)


def _dot_nt(a, b):
    return lax.dot_general(a, b, (((1,), (1,)), ((), ())), preferred_element_type=F32)


def _dot_x01(x, m01):
    hi, lo = _split_bf16(x)
    return _dot(hi, m01) + _dot(lo, m01)


def _dot_01x(m01, x):
    hi, lo = _split_bf16(x)
    return _dot(m01, hi) + _dot(m01, lo)


def _idiv(x, c):
    return lax.shift_right_logical(x, c.bit_length() - 1) if c & (c - 1) == 0 else x // c


def _imod(x, c):
    return (x & (c - 1)) if c & (c - 1) == 0 else x % c


def _head_sum_bcast(x, sel, exp):
    return _dot_x01(_dot_x01(x, sel), exp)


def _softplus(z):
    return jnp.maximum(z, 0.0) + jnp.log1p(jnp.exp(-jnp.abs(z)))


def _sigmoid(z):
    return 1.0 / (1.0 + jnp.exp(-z))


def _ln_rows(x, g, b):
    mu = jnp.mean(x, axis=-1, keepdims=True)
    xc = x - mu
    var = jnp.mean(xc * xc, axis=-1, keepdims=True)
    return xc * lax.rsqrt(var + LN_EPS) * g + b


def _ln0_kernel(x_ref, g_ref, b_ref, o_ref):
    o_ref[...] = _ln_rows(x_ref[...], g_ref[...], b_ref[...]).astype(o_ref.dtype)


def _ln0(x2, g, b):
    m, d = x2.shape
    tm = _pick(m, (528, 512, 384, 256, 128, 64, 32, 16, 8))
    return pl.pallas_call(
        _ln0_kernel,
        grid=(m // tm,),
        in_specs=[pl.BlockSpec((tm, d), lambda i: (i, 0)),
                  pl.BlockSpec((1, d), lambda i: (0, 0)),
                  pl.BlockSpec((1, d), lambda i: (0, 0))],
        out_specs=pl.BlockSpec((tm, d), lambda i: (i, 0)),
        out_shape=jax.ShapeDtypeStruct((m, d), BF16),
        compiler_params=_cparams(("parallel",)),
        name="ln0",
    )(x2, g.reshape(1, d), b.reshape(1, d))


def _ln1_kernel(x_ref, mix_ref, g0_ref, b0_ref, g_ref, b_ref, of_ref, ob_ref, *, alpha):
    h = _ln_rows(x_ref[...], g0_ref[...], b0_ref[...])
    x1 = _ln_rows(alpha * h + mix_ref[...], g_ref[...], b_ref[...])
    of_ref[...] = x1
    ob_ref[...] = x1.astype(BF16)


def _ln1(x2, mix, g0, b0, g, b, alpha):
    m, d = x2.shape
    tm = _pick(m, (264, 256, 128, 64, 32, 16, 8))
    row = pl.BlockSpec((tm, d), lambda i: (i, 0))
    vec = pl.BlockSpec((1, d), lambda i: (0, 0))
    return pl.pallas_call(
        functools.partial(_ln1_kernel, alpha=alpha),
        grid=(m // tm,),
        in_specs=[row, row, vec, vec, vec, vec],
        out_specs=[row, row],
        out_shape=[jax.ShapeDtypeStruct((m, d), F32), jax.ShapeDtypeStruct((m, d), BF16)],
        compiler_params=_cparams(("parallel",)),
        name="ln1",
    )(x2, mix, g0.reshape(1, d), b0.reshape(1, d), g.reshape(1, d), b.reshape(1, d))


def _ln2_kernel(x1_ref, ffn_ref, g_ref, b_ref, o_ref, *, alpha):
    o_ref[...] = _ln_rows(alpha * x1_ref[...] + ffn_ref[...], g_ref[...], b_ref[...])


def _ln2(x1, ffn, g, b, alpha, n_seq, seq_rows, skip_rows):
    d = x1.shape[-1]
    out_rows = seq_rows - skip_rows
    tb = next(c for c in (256, 128, 64, 32, 16, 8) if skip_rows % c == 0 and out_rows % c == 0)
    skip = skip_rows // tb
    x3 = x1.reshape(n_seq, seq_rows, d)
    f3 = ffn.reshape(n_seq, seq_rows, d)
    row_in = pl.BlockSpec((None, tb, d), lambda s, i: (s, i + skip, 0))
    vec = pl.BlockSpec((1, d), lambda s, i: (0, 0))
    return pl.pallas_call(
        functools.partial(_ln2_kernel, alpha=alpha),
        grid=(n_seq, out_rows // tb),
        in_specs=[row_in, row_in, vec, vec],
        out_specs=pl.BlockSpec((None, tb, d), lambda s, i: (s, i, 0)),
        out_shape=jax.ShapeDtypeStruct((n_seq, out_rows, d), F32),
        compiler_params=_cparams(("parallel", "parallel")),
        name="ln2",
    )(x3, f3, g.reshape(1, d), b.reshape(1, d))


def _mm_kernel(x_ref, w_ref, o_ref):
    o_ref[...] = _dot(x_ref[...], w_ref[...])


def _matmul(x, w, tm, tn, name):
    m, k = x.shape
    n = w.shape[1]
    return pl.pallas_call(
        _mm_kernel,
        grid=(m // tm, n // tn),
        in_specs=[pl.BlockSpec((tm, k), lambda i, j: (i, 0)),
                  pl.BlockSpec((k, tn), lambda i, j: (0, j))],
        out_specs=pl.BlockSpec((tm, tn), lambda i, j: (i, j)),
        out_shape=jax.ShapeDtypeStruct((m, n), F32),
        compiler_params=_cparams(("parallel", "arbitrary")),
        name=name,
    )(x, w)


def _mm2_kernel(xa_ref, xb_ref, wa_ref, wb_ref, o_ref):
    o_ref[...] = _dot(xa_ref[...], wa_ref[...]) + _dot(xb_ref[...], wb_ref[...])


def _matmul2(xa, xb, w, tm, tn, name):
    m, ka = xa.shape
    kb = xb.shape[1]
    n = w.shape[1]
    assert ka == kb, "the weight's second row block is addressed as block index 1"
    return pl.pallas_call(
        _mm2_kernel,
        grid=(m // tm, n // tn),
        in_specs=[pl.BlockSpec((tm, ka), lambda i, j: (i, 0)),
                  pl.BlockSpec((tm, kb), lambda i, j: (i, 0)),
                  pl.BlockSpec((ka, tn), lambda i, j: (0, j)),
                  pl.BlockSpec((kb, tn), lambda i, j: (1, j))],
        out_specs=pl.BlockSpec((tm, tn), lambda i, j: (i, j)),
        out_shape=jax.ShapeDtypeStruct((m, n), F32),
        compiler_params=_cparams(("parallel", "arbitrary")),
        name=name,
    )(xa, xb, w, w)


CHAINS = LANES // 2
KH = HEAD_DIM // 2
KQ = KH // 2
VQ = HEAD_DIM // 4


def _rwkv_col_orders(n_heads):
    assert 4 * n_heads == LANES
    c = np.arange(n_heads * HEAD_DIM)
    h, rest = c % n_heads, c // n_heads
    pk = h * HEAD_DIM + ((rest // 2) % 2) * KH + (rest % 2) * KQ + rest // 4
    pv = h * HEAD_DIM + (rest % 4) * VQ + rest // 4
    return pk, pv


def _head_select_minor(n_cols, n_heads):
    c = lax.broadcasted_iota(jnp.int32, (n_cols, n_heads), 0)
    h = lax.broadcasted_iota(jnp.int32, (n_cols, n_heads), 1)
    return jnp.where(_imod(c, n_heads) == h, 1.0, 0.0).astype(BF16)


def _head_expand_minor(n_heads, n_cols):
    h = lax.broadcasted_iota(jnp.int32, (n_heads, n_cols), 0)
    c = lax.broadcasted_iota(jnp.int32, (n_heads, n_cols), 1)
    return jnp.where(_imod(c, n_heads) == h, 1.0, 0.0).astype(BF16)


def _pair_to_chain_k(x0, x1):
    tm = x0.shape[0]
    a = x0.reshape(tm, KQ, LANES)
    b = x1.reshape(tm, KQ, LANES)
    lane = lax.broadcasted_iota(jnp.int32, (tm, KQ, LANES), 2)
    seq0 = _imod(lane, CHAINS) < CHAINS // 2
    t0 = jnp.where(seq0, a, pltpu.roll(b, CHAINS // 2, 2))
    t1 = jnp.where(seq0, pltpu.roll(a, LANES - CHAINS // 2, 2), b)
    return jnp.concatenate([t0, t1], axis=1)


def _pair_to_chain_v(x0, x1):
    tm = x0.shape[0]
    a = x0.reshape(tm, VQ, LANES)
    b = x1.reshape(tm, VQ, LANES)
    q = CHAINS // 2
    grp = _idiv(lax.broadcasted_iota(jnp.int32, (tm, VQ, LANES), 2), q)
    outs = []
    for j in range(4):
        pa = jnp.where(grp == j, a, 0.0)
        pb = jnp.where(grp == j, b, 0.0)
        sa, sb = (LANES - q * j) % LANES, (q - q * j) % LANES
        y = (pltpu.roll(pa, sa, 2) if sa else pa) + (pltpu.roll(pb, sb, 2) if sb else pb)
        outs.append(y + pltpu.roll(y, CHAINS, 2))
    return jnp.concatenate(outs, axis=1)


def _chain_to_pair_v(o):
    tm = o.shape[0]
    q = CHAINS // 2
    grp = _idiv(lax.broadcasted_iota(jnp.int32, (tm, VQ, LANES), 2), q)
    res = []
    for s in range(2):
        acc = jnp.zeros((tm, VQ, LANES), F32)
        for j in range(4):
            piece = o[:, j * VQ:(j + 1) * VQ, :]
            sh = (q * j - q * s) % LANES
            acc = jnp.where(grp == j, pltpu.roll(piece, sh, 2) if sh else piece, acc)
        res.append(acc.reshape(tm, VQ * LANES))
    return res


def _rwkv_prep_kernel(*refs, pad, tm, dr, n_heads, sample):
    if sample:
        (rkv_ref, lora_ref, sp_rkv_ref, sp_lora_ref, mu_rkv_ref, mu_lora_ref, wd_ref, wa_ref, wg_ref,
         w0_ref, a0_ref, kk_s_ref, ka_s_ref, rk_ref,
         r_o, w_o, k_o, v_o, kk_o, b_o, g_o, bon_o) = refs
    else:
        (rkv_ref, lora_ref, mu_rkv_ref, mu_lora_ref, wd_ref, wa_ref, wg_ref,
         w0_ref, a0_ref, kk_s_ref, ka_s_ref, rk_ref,
         r_o, w_o, k_o, v_o, kk_o, b_o, g_o, bon_o, carry_rkv, carry_lora) = refs
    i = pl.program_id(1)
    row = lax.broadcasted_iota(jnp.int32, (tm, 1), 0)
    sel = _head_select_minor(dr, n_heads)
    exp = _head_expand_minor(n_heads, dr)

    if not sample:
        @pl.when(i == 0)
        def _():
            carry_rkv[...] = jnp.zeros_like(carry_rkv)
            carry_lora[...] = jnp.zeros_like(carry_lora)

    per_seq = []
    for s in range(2):
        p_rkv = rkv_ref[s]
        p_lora = lora_ref[s]
        if sample:
            prev_rkv = jnp.where(row == 0, sp_rkv_ref[s], pltpu.roll(p_rkv, 1, 0))
            prev_lora = jnp.where(row == 0, sp_lora_ref[s], pltpu.roll(p_lora, 1, 0))
        else:
            pos = i * tm + row
            real = pos >= pad
            p_rkv = jnp.where(real, p_rkv, 0.0)
            p_lora = jnp.where(real, p_lora, 0.0)
            prev_rkv = jnp.where(row == 0, carry_rkv[s, 0:1, :], pltpu.roll(p_rkv, 1, 0))
            prev_lora = jnp.where(row == 0, carry_lora[s, 0:1, :], pltpu.roll(p_lora, 1, 0))
            prev_rkv = jnp.where(pos > pad, prev_rkv, 0.0)
            prev_lora = jnp.where(pos > pad, prev_lora, 0.0)
            carry_rkv[s, 0:1, :] = p_rkv[tm - 1:tm, :]
            carry_lora[s, 0:1, :] = p_lora[tm - 1:tm, :]

        xs_rkv = p_rkv + (prev_rkv - p_rkv) * mu_rkv_ref[...]
        xl = p_lora + (prev_lora - p_lora) * mu_lora_ref[...]
        r = xs_rkv[:, 0:dr]
        k = xs_rkv[:, dr:2 * dr]
        v = xs_rkv[:, 2 * dr:3 * dr]

        dec = _dot(jnp.tanh(xl).astype(BF16), wd_ref[...])
        a_pre = _dot(xl.astype(BF16), wa_ref[...])
        g = _dot(_sigmoid(xl).astype(BF16), wg_ref[...])
        w_log = -_softplus(-(w0_ref[...] + dec)) - 0.5
        decay = jnp.exp(-jnp.exp(w_log))
        a = _sigmoid(a0_ref[...] + a_pre)

        kk = k * kk_s_ref[...]
        nrm = jnp.sqrt(_head_sum_bcast(kk * kk, sel, exp))
        kk = kk / jnp.maximum(nrm, 1e-12)
        k_mod = k * (1.0 + (a - 1.0) * ka_s_ref[...])
        g_o[s] = g
        bon_o[s] = _head_sum_bcast(r * k_mod * rk_ref[...], sel, exp) * v
        per_seq.append((r, decay, k_mod, kk, kk * a, v))

    for idx, out in enumerate((r_o, w_o, k_o, kk_o, b_o)):
        out[...] = _pair_to_chain_k(per_seq[0][idx], per_seq[1][idx])
    v_o[...] = _pair_to_chain_v(per_seq[0][5], per_seq[1][5])


def _rwkv_prep(p3, sp_rkv, sp_lora, prm, *, pad, sample):
    n_seq, t, _ = p3.shape
    assert n_seq % 2 == 0
    n_pairs = n_seq // 2
    dr = prm["dr"]
    dl = prm["lora_w"]
    tm = t if sample else _pick(t, (64, 32, 16, 8))
    lora_blk = (3 * dr + 3 * prm["df"]) // dl
    full = lambda a: pl.BlockSpec(a.shape, lambda p, i: (0,) * a.ndim)
    consts = [prm["mu_rkv"], prm["mu_lora"], prm["wd"], prm["wa"], prm["wg"], prm["w0"], prm["a0"],
              prm["k_k"], prm["k_a"], prm["r_k"]]
    in_specs = [pl.BlockSpec((2, tm, 3 * dr), lambda p, i: (p, i, 0)),
                pl.BlockSpec((2, tm, dl), lambda p, i: (p, i, lora_blk))]
    args = [p3, p3]
    if sample:
        in_specs += [pl.BlockSpec((2, tm, 3 * dr), lambda p, i: (p, i, 0)),
                     pl.BlockSpec((2, tm, dl), lambda p, i: (p, i, 0))]
        args += [sp_rkv, sp_lora]
    in_specs += [full(c) for c in consts]
    args += consts
    kspec = pl.BlockSpec((tm, KH, LANES), lambda p, i: (i, 0, p))
    vspec = pl.BlockSpec((tm, HEAD_DIM, LANES), lambda p, i: (i, 0, p))
    rspec = pl.BlockSpec((2, tm, dr), lambda p, i: (p, i, 0))
    kshape = jax.ShapeDtypeStruct((t, KH, n_pairs * LANES), F32)
    vshape = jax.ShapeDtypeStruct((t, HEAD_DIM, n_pairs * LANES), F32)
    rshape = jax.ShapeDtypeStruct((n_seq, t, dr), F32)
    scratch = [] if sample else [pltpu.VMEM((2, 8, 3 * dr), F32), pltpu.VMEM((2, 8, dl), F32)]
    return pl.pallas_call(
        functools.partial(_rwkv_prep_kernel, pad=pad, tm=tm, dr=dr, n_heads=dr // HEAD_DIM, sample=sample),
        grid=(n_pairs, t // tm),
        in_specs=in_specs,
        out_specs=[kspec, kspec, kspec, vspec, kspec, kspec, rspec, rspec],
        out_shape=[kshape, kshape, kshape, vshape, kshape, kshape, rshape, rshape],
        scratch_shapes=scratch,
        compiler_params=_cparams(("parallel", "arbitrary")),
        name="rwkv_prep_sample" if sample else "rwkv_prep_prompt",
    )(*args)


def _rwkv_scan_kernel(r_ref, w_ref, k_ref, kk_ref, b_ref, v_ref, s0_ref, o_ref, sf_ref, s_scr, *, tb):
    t_blk = pl.program_id(1)

    @pl.when(t_blk == 0)
    def _():
        s_scr[...] = s0_ref[...]

    def step(t, carry):
        r = r_ref[t]
        w = w_ref[t]
        k = k_ref[t]
        kk = kk_ref[t]
        b = b_ref[t]
        for vi in range(HEAD_DIM):
            sv = s_scr[vi]
            skk = jnp.sum(sv * kk, axis=0, keepdims=True)
            skk = skk + pltpu.roll(skk, CHAINS, 1)
            vrow = v_ref[t, pl.ds(vi, 1), :]
            sv = sv * w - skk * b + vrow * k
            s_scr[vi] = sv
            o = jnp.sum(sv * r, axis=0, keepdims=True)
            o_ref[t, pl.ds(vi, 1), :] = o + pltpu.roll(o, CHAINS, 1)
        return carry

    lax.fori_loop(0, tb, step, 0, unroll=2 if tb % 2 == 0 else 1)

    @pl.when(t_blk == pl.num_programs(1) - 1)
    def _():
        sf_ref[...] = s_scr[...]


def _state_to_chain_layout(s):
    n_seq, h = s.shape[:2]
    g = (n_seq * h) // CHAINS
    s = s.reshape(g, CHAINS, HEAD_DIM, 2, KH).transpose(2, 4, 0, 3, 1)
    return s.reshape(HEAD_DIM, KH, g * LANES)


def _state_from_chain_layout(s, n_seq, h):
    g = (n_seq * h) // CHAINS
    s = s.reshape(HEAD_DIM, KH, g, 2, CHAINS).transpose(2, 4, 0, 3, 1)
    return s.reshape(n_seq, h, HEAD_DIM, HEAD_DIM)


def _rwkv_scan(r, w, k, kk, b, v, s0):
    n_seq, h = s0.shape[:2]
    t = r.shape[0]
    g = (n_seq * h) // CHAINS
    tb = _pick(t, (64, 32, 16, 8, 4, 2, 1))
    ins = [r, w, k, kk, b]
    vin = v
    sin = _state_to_chain_layout(s0)
    kspec = pl.BlockSpec((tb, KH, LANES), lambda gi, ti: (ti, 0, gi))
    vspec = pl.BlockSpec((tb, HEAD_DIM, LANES), lambda gi, ti: (ti, 0, gi))
    sspec = pl.BlockSpec((HEAD_DIM, KH, LANES), lambda gi, ti: (0, 0, gi))
    o, sf = pl.pallas_call(
        functools.partial(_rwkv_scan_kernel, tb=tb),
        grid=(g, t // tb),
        in_specs=[kspec] * 5 + [vspec, sspec],
        out_specs=[vspec, sspec],
        out_shape=[jax.ShapeDtypeStruct((t, HEAD_DIM, g * LANES), F32),
                   jax.ShapeDtypeStruct((HEAD_DIM, KH, g * LANES), F32)],
        scratch_shapes=[pltpu.VMEM((HEAD_DIM, KH, LANES), F32)],
        compiler_params=_cparams(("parallel", "arbitrary")),
        name="rwkv_scan",
    )(*ins, vin, sin)
    return o, _state_from_chain_layout(sf, n_seq, h)


def _rwkv_post_kernel(o_ref, g_ref, bon_ref, gg_ref, gb_ref, out_ref, *, dr):
    n_heads = dr // HEAD_DIM
    sel = _head_select_minor(dr, n_heads)
    exp = _head_expand_minor(n_heads, dr)
    for s, o in enumerate(_chain_to_pair_v(o_ref[...])):
        mu = _head_sum_bcast(o, sel, exp) * (1.0 / HEAD_DIM)
        oc = o - mu
        var = _head_sum_bcast(oc * oc, sel, exp) * (1.0 / HEAD_DIM)
        y = oc * lax.rsqrt(var + GN_EPS) * gg_ref[...] + gb_ref[...]
        out_ref[s] = ((y + bon_ref[s]) * g_ref[s]).astype(BF16)


def _rwkv_post(o, g, bonus, gn_g, gn_b):
    n_seq, t, dr = g.shape
    tm = t if t <= 64 else _pick(t, (64, 32, 16, 8))
    row = pl.BlockSpec((2, tm, dr), lambda p, i: (p, i, 0))
    vec = pl.BlockSpec((1, dr), lambda p, i: (0, 0))
    return pl.pallas_call(
        functools.partial(_rwkv_post_kernel, dr=dr),
        grid=(n_seq // 2, t // tm),
        in_specs=[pl.BlockSpec((tm, HEAD_DIM, LANES), lambda p, i: (i, 0, p)), row, row, vec, vec],
        out_specs=row,
        out_shape=jax.ShapeDtypeStruct((n_seq, t, dr), BF16),
        compiler_params=_cparams(("parallel", "parallel")),
        name="rwkv_post",
    )(o, g, bonus, gn_g.reshape(1, dr), gn_b.reshape(1, dr))


def _logf_kernel(lora_ref, bf_ref, lf_ref, c_ref, carry, *, tb, pad, f_off, n_heads):
    i = pl.program_id(1)

    @pl.when(i == 0)
    def _():
        carry[...] = jnp.zeros_like(carry)

    z = lora_ref[:, f_off:f_off + n_heads] + bf_ref[...]
    lf = -_softplus(-z)
    pos = i * tb + lax.broadcasted_iota(jnp.int32, (tb, 1), 0)
    lf = jnp.where(pos >= pad, lf, 0.0)
    rr = lax.broadcasted_iota(jnp.int32, (tb, tb), 0)
    cc = lax.broadcasted_iota(jnp.int32, (tb, tb), 1)
    tril = jnp.where(cc <= rr, 1.0, 0.0).astype(BF16)
    c = _dot_01x(tril, lf) + carry[0:1, :]
    lf_ref[...] = lf
    c_ref[...] = c
    carry[0:1, :] = c[tb - 1:tb, :]


def _logf_cumsum(p, b_f, prm, n_seq, seq_rows, pad):
    dl = prm["lora_w"]
    hf = b_f.shape[-1]
    tb = _pick(seq_rows, (128, 64, 32, 16, 8))
    lora_blk = (3 * prm["dr"] + 3 * prm["df"]) // dl
    p3 = p.reshape(n_seq, seq_rows, p.shape[-1])
    out = pl.BlockSpec((None, tb, hf), lambda s, i: (s, i, 0))
    return pl.pallas_call(
        functools.partial(_logf_kernel, tb=tb, pad=pad, f_off=prm["f_off"], n_heads=hf),
        grid=(n_seq, seq_rows // tb),
        in_specs=[pl.BlockSpec((None, tb, dl), lambda s, i: (s, i, lora_blk)),
                  pl.BlockSpec((1, hf), lambda s, i: (0, 0))],
        out_specs=[out, out],
        out_shape=[jax.ShapeDtypeStruct((n_seq, seq_rows, hf), F32)] * 2,
        scratch_shapes=[pltpu.VMEM((8, hf), F32)],
        compiler_params=_cparams(("parallel", "arbitrary")),
        name="logf_cumsum",
    )(p3, b_f.reshape(1, hf))


LOG2E = 1.4426950408889634


def _bf16_terms(x):
    hi = x.astype(BF16)
    r = x - hi.astype(F32)
    mid = r.astype(BF16)
    lo = (r - mid.astype(F32)).astype(BF16)
    return hi.astype(F32), mid.astype(F32), lo.astype(F32)


def _augment(x, own, lane, first, ones_at, terms):
    t0, t1, t2 = terms
    a = first
    extra = jnp.where(lane == a + (3 - ones_at), t0, jnp.where(lane == a + (4 - ones_at), t1,
            jnp.where(lane == a + (5 - ones_at), t2, 0.0)))
    extra = jnp.where((lane >= a + ones_at) & (lane < a + ones_at + 3), 1.0, extra)
    return jnp.where(own, x, extra).astype(BF16)


def _fox_prompt_kernel(q_ref, k_ref, v_ref, cq_ref, ck_ref, o_ref, ka_scr, va_scr, m_scr, acc_scr, *, tq, scale2):
    i = pl.program_id(2)
    seq_rows = k_ref.shape[0]

    @pl.when(i == 0)
    def _():
        lane_k = lax.broadcasted_iota(jnp.int32, (seq_rows, LANES), 1)
        k = k_ref[...]
        v = v_ref[...]
        for h in range(2):
            own = (lane_k < HEAD_DIM) if h == 0 else (lane_k >= HEAD_DIM)
            first = HEAD_DIM * (1 - h)
            ka_scr[h] = _augment(k, own, lane_k, first, 0, _bf16_terms(-ck_ref[:, h:h + 1]))
            va_scr[h] = jnp.where(own, v, 1.0).astype(BF16)

    lane = lax.broadcasted_iota(jnp.int32, (tq, LANES), 1)
    q = q_ref[...] * scale2
    qa = []
    for h in range(2):
        own = (lane < HEAD_DIM) if h == 0 else (lane >= HEAD_DIM)
        qa.append(_augment(q, own, lane, HEAD_DIM * (1 - h), 3, _bf16_terms(cq_ref[:, h:h + 1])))
    m_scr[...] = jnp.full_like(m_scr, NEG_INF)
    acc_scr[...] = jnp.zeros_like(acc_scr)

    def block(j, width, diagonal):
        off = pl.multiple_of(j * tq, tq)
        for h in range(2):
            s = _dot_nt(qa[h], ka_scr[h, pl.ds(off, width), :])
            if diagonal:
                rr = lax.broadcasted_iota(jnp.int32, (tq, width), 0)
                cc = lax.broadcasted_iota(jnp.int32, (tq, width), 1)
                s = jnp.where(cc <= rr, s, NEG_INF)
            cols = [s[:, c * LANES:(c + 1) * LANES] for c in range(width // LANES)]
            m_blk = jnp.max(functools.reduce(jnp.maximum, cols), axis=-1, keepdims=True)
            m_prev = m_scr[h]
            m_new = jnp.maximum(m_prev, m_blk)
            p = jnp.concatenate([jnp.exp2(c - m_new) for c in cols], axis=1).astype(BF16)
            acc_scr[h] = jnp.exp2(m_prev - m_new) * acc_scr[h] + _dot(p, va_scr[h, pl.ds(off, width), :])
            m_scr[h] = m_new

    def body(jj, carry):
        block(2 * jj, 2 * tq, False)
        return carry

    lax.fori_loop(0, i // 2, body, 0)

    @pl.when(i % 2 == 1)
    def _():
        block(i - 1, tq, False)

    block(i, tq, True)
    a0 = acc_scr[0]
    a1 = acc_scr[1]
    o = jnp.where(lane < HEAD_DIM, a0 / pltpu.roll(a0, HEAD_DIM, 1), a1 / pltpu.roll(a1, HEAD_DIM, 1))
    o_ref[...] = o.astype(BF16)


def _fox_prompt(p, c, prm, n_seq, seq_rows, pad):
    dr, df = prm["dr"], prm["df"]
    hf = df // HEAD_DIM
    n_pairs = hf // 2
    tq = _pick(seq_rows, (384, 256, 128))
    nq = seq_rows // tq
    p3 = p.reshape(n_seq, seq_rows, p.shape[-1])
    c4 = (c * LOG2E).reshape(n_seq, seq_rows, n_pairs, 2)
    cq = c4.transpose(0, 2, 1, 3)
    is_pad = (jnp.arange(seq_rows) < pad)[None, :, None, None]
    ck = jnp.where(is_pad, -NEG_INF, c4).transpose(0, 2, 1, 3)
    q0, k0, v0 = (3 * dr) // LANES, (3 * dr + df) // LANES, (3 * dr + 2 * df) // LANES
    return pl.pallas_call(
        functools.partial(_fox_prompt_kernel, tq=tq, scale2=HEAD_DIM ** -0.5 * LOG2E),
        grid=(n_seq, n_pairs, nq),
        in_specs=[pl.BlockSpec((None, tq, LANES), lambda b, pr, i: (b, i, q0 + pr)),
                  pl.BlockSpec((None, seq_rows, LANES), lambda b, pr, i: (b, 0, k0 + pr)),
                  pl.BlockSpec((None, seq_rows, LANES), lambda b, pr, i: (b, 0, v0 + pr)),
                  pl.BlockSpec((None, None, tq, 2), lambda b, pr, i: (b, pr, i, 0)),
                  pl.BlockSpec((None, None, seq_rows, 2), lambda b, pr, i: (b, pr, 0, 0))],
        out_specs=pl.BlockSpec((None, tq, LANES), lambda b, pr, i: (b, i, pr)),
        out_shape=jax.ShapeDtypeStruct((n_seq, seq_rows, df), BF16),
        scratch_shapes=[pltpu.VMEM((2, seq_rows, LANES), BF16), pltpu.VMEM((2, seq_rows, LANES), BF16),
                        pltpu.VMEM((2, tq, LANES), F32), pltpu.VMEM((2, tq, LANES), F32)],
        compiler_params=_cparams(("parallel", "parallel", "arbitrary")),
        name="fox_prompt",
    )(p3, p3, p3, cq, ck).reshape(n_seq * seq_rows, df)


def _fox_sample_kernel(pt_ref, q_ref, kn_ref, vn_ref, cn_ref, cnrow_ref, *rest, pp, page, nq, hf, scale):
    lf_refs = rest[:pp]
    k_refs = rest[pp:2 * pp]
    v_refs = rest[2 * pp:3 * pp]
    o_ref, qbd_scr, m_scr, l_scr, acc_scr, carry_scr = rest[3 * pp:]
    step = pl.program_id(1)
    gh = MXU_DEPTH // HEAD_DIM
    ng = hf // gh
    gr = gh * nq
    gd = gh * HEAD_DIM
    rows = hf * nq
    own = (_idiv(lax.broadcasted_iota(jnp.int32, (gr, gd), 0), nq)
           == _idiv(lax.broadcasted_iota(jnp.int32, (gr, gd), 1), HEAD_DIM))

    @pl.when(step == 0)
    def _():
        q = q_ref[...] * scale
        for g in range(ng):
            qg = jnp.broadcast_to(q[None, :, g * gd:(g + 1) * gd], (gh, nq, gd)).reshape(gr, gd)
            qbd_scr[g] = jnp.where(own, qg, 0.0).astype(BF16)
        m_scr[...] = jnp.full_like(m_scr, NEG_INF)
        l_scr[...] = jnp.zeros_like(l_scr)
        acc_scr[...] = jnp.zeros_like(acc_scr)
        carry_scr[...] = jnp.zeros_like(carry_scr)

    tr = lax.broadcasted_iota(jnp.int32, (page, 2 * page), 0)
    tc = lax.broadcasted_iota(jnp.int32, (page, 2 * page), 1)
    suffix = jnp.where((tc >= page) | (tr > tc), 1.0, 0.0).astype(BF16)
    carry = carry_scr[...]
    later_sum = [None] * pp
    for x in reversed(range(pp)):
        r = _dot_x01(lf_refs[x][...], suffix)
        later_sum[x] = r[:, :page] + carry
        carry = carry + r[:, page:]
    carry_scr[...] = carry
    bias = jnp.concatenate(later_sum, axis=1)
    bias = jnp.broadcast_to(bias[:, None, :], (hf, nq, pp * page)).reshape(rows, pp * page)

    group = lambda ref, g: ref[g * gd:(g + 1) * gd, :].astype(BF16)
    s = jnp.concatenate(
        [jnp.concatenate([_dot(qbd_scr[g], group(k_refs[x], g)) for g in range(ng)], axis=0) for x in range(pp)],
        axis=1)
    s = s + cn_ref[...] + bias
    m_prev = m_scr[...]
    m_new = jnp.maximum(m_prev, jnp.max(s, axis=-1, keepdims=True))
    alpha = jnp.exp(m_prev - m_new)
    p = jnp.exp(s - m_new)
    l_scr[...] = alpha * l_scr[...] + jnp.sum(p, axis=-1, keepdims=True)
    m_scr[...] = m_new
    for g in range(ng):
        rws = slice(g * gr, (g + 1) * gr)
        acc = alpha[rws] * acc_scr[g]
        for x in range(pp):
            acc = acc + _dot_nt(p[rws, x * page:(x + 1) * page].astype(BF16), group(v_refs[x], g))
        acc_scr[g] = acc

    @pl.when(step == pl.num_programs(1) - 1)
    def _():
        q = (q_ref[...] * scale).astype(BF16)
        kn = kn_ref[...].astype(BF16)
        vn = vn_ref[...].astype(BF16)
        head = lambda a, h: a[:, h * HEAD_DIM:(h + 1) * HEAD_DIM]
        sn = jnp.stack([_dot_nt(head(q, h), head(kn, h)) for h in range(hf)])
        qi = lax.broadcasted_iota(jnp.int32, (hf, nq, nq), 1)
        ki = lax.broadcasted_iota(jnp.int32, (hf, nq, nq), 2)
        sn = jnp.where(ki <= qi, sn + cnrow_ref[...], NEG_INF).reshape(rows, nq)
        m_last = m_scr[...]
        m_fin = jnp.maximum(m_last, jnp.max(sn, axis=-1, keepdims=True))
        a_fin = jnp.exp(m_last - m_fin)
        pn = jnp.exp(sn - m_fin)
        inv = 1.0 / (a_fin * l_scr[...] + jnp.sum(pn, axis=-1, keepdims=True))
        pn = (pn * inv).astype(BF16)
        o_new = jnp.concatenate([_dot(pn[h * nq:(h + 1) * nq, :], head(vn, h)) for h in range(hf)], axis=1)
        w_past = a_fin * inv
        o_past = jnp.concatenate(
            [jnp.sum(jnp.where(own, acc_scr[g] * w_past[g * gr:(g + 1) * gr], 0.0).reshape(gh, nq, gd), axis=0)
             for g in range(ng)], axis=1)
        o_ref[...] = (o_past + o_new).astype(BF16)


def _fox_sample(p_s, c_new, cache_k, cache_v, cache_lft, page_table_flat, prm, n_seq, nq, n_pages, layer):
    dr, df = prm["dr"], prm["df"]
    hf = df // HEAD_DIM
    page = cache_k.shape[3]
    gh = MXU_DEPTH // HEAD_DIM
    assert hf % gh == 0
    pp = _pick(n_pages, (4, 2, 1))
    npb = n_pages // pp
    rows = hf * nq
    p3 = p_s.reshape(n_seq, nq, p_s.shape[-1])
    cn_t = c_new.transpose(0, 2, 1)
    cn = cn_t.reshape(n_seq, rows, 1)
    cn_row = cn_t[..., None] - cn_t[:, :, None, :]
    qb, kb, vb = (3 * dr) // df, (3 * dr + df) // df, (3 * dr + 2 * df) // df
    seq_blk = lambda blk: pl.BlockSpec((None, nq, df), lambda b, i, pt: (b, 0, blk))

    def page_of(b, i, pt, x):
        return pt[b * n_pages + (npb - 1 - i) * pp + x]

    def kv_spec(x):
        return pl.BlockSpec((None, None, df, page), lambda b, i, pt: (layer, page_of(b, i, pt, x), 0, 0))

    def lf_spec(x):
        return pl.BlockSpec((None, None, hf, page), lambda b, i, pt: (layer, page_of(b, i, pt, x), 0, 0))

    grid_spec = pltpu.PrefetchScalarGridSpec(
        num_scalar_prefetch=1,
        grid=(n_seq, npb),
        in_specs=[seq_blk(qb), seq_blk(kb), seq_blk(vb),
                  pl.BlockSpec((None, rows, 1), lambda b, i, pt: (b, 0, 0)),
                  pl.BlockSpec((None, hf, nq, nq), lambda b, i, pt: (b, 0, 0, 0))]
                 + [lf_spec(x) for x in range(pp)] + [kv_spec(x) for x in range(pp)] * 2,
        out_specs=pl.BlockSpec((None, nq, df), lambda b, i, pt: (b, 0, 0)),
        scratch_shapes=[pltpu.VMEM((hf // gh, gh * nq, MXU_DEPTH), BF16), pltpu.VMEM((rows, 1), F32),
                        pltpu.VMEM((rows, 1), F32), pltpu.VMEM((hf // gh, gh * nq, MXU_DEPTH), F32),
                        pltpu.VMEM((hf, page), F32)],
    )
    out = pl.pallas_call(
        functools.partial(_fox_sample_kernel, pp=pp, page=page, nq=nq, hf=hf, scale=HEAD_DIM ** -0.5),
        grid_spec=grid_spec,
        out_shape=jax.ShapeDtypeStruct((n_seq, nq, df), BF16),
        compiler_params=_cparams(("parallel", "arbitrary")),
        name="fox_sample",
    )(page_table_flat, p3, p3, p3, cn, cn_row, *([cache_lft] * pp), *([cache_k] * pp), *([cache_v] * pp))
    return out.reshape(n_seq * nq, df)


def _gelu_tanh(x):
    return 0.5 * x * (1.0 + jnp.tanh(0.7978845608028654 * (x + 0.044715 * (x * x * x))))


def _ffn_in_kernel(*refs, tm, tn, seq_rows, pad, sample):
    if sample:
        x_ref, wu_ref, wg_ref, cw_ref, cb_ref, p1_ref, p2_ref, h_ref, u_ref = refs
    else:
        x_ref, wu_ref, wg_ref, cw_ref, cb_ref, h_ref, tail_ref, carry = refs
    i = pl.program_id(0)
    j = pl.program_id(1)
    x = x_ref[...]
    u = _dot(x, wu_ref[...].astype(BF16))
    gate = _dot(x, wg_ref[...].astype(BF16))
    row = lax.broadcasted_iota(jnp.int32, (tm, 1), 0)
    if sample:
        t = _imod(row, seq_rows)
        u1 = jnp.where(t == 0, p1_ref[...], pltpu.roll(u, 1, 0))
        u2 = jnp.where(t < 2, p2_ref[...], pltpu.roll(u, 2, 0))
        u_ref[...] = u
    else:
        @pl.when(i == 0)
        def _():
            carry[:, pl.ds(pl.multiple_of(j * tn, tn), tn)] = jnp.zeros((8, tn), F32)

        pos = (i * tm) % seq_rows + row
        u = jnp.where(pos >= pad, u, 0.0)
        prev = carry[:, pl.ds(pl.multiple_of(j * tn, tn), tn)]
        u1 = jnp.where(row == 0, prev[7:8, :], pltpu.roll(u, 1, 0))
        u2 = jnp.where(row == 0, prev[6:7, :], jnp.where(row == 1, prev[7:8, :], pltpu.roll(u, 2, 0)))
        tail = u[tm - 8:tm, :]
        carry[:, pl.ds(pl.multiple_of(j * tn, tn), tn)] = tail
        tail_ref[...] = tail
    c = cb_ref[...] + cw_ref[0:1, :] * u2 + cw_ref[1:2, :] * u1 + cw_ref[2:3, :] * u
    h_ref[...] = (_gelu_tanh(c) * gate).astype(BF16)


def _ffn_in(x1b, w_ffn_in, conv_w, conv_b, p1, p2, *, seq_rows, pad, sample):
    m, d = x1b.shape
    dff = conv_w.shape[-1]
    tn = _pick(dff, (256, 128))
    nj = dff // tn
    tm = m if sample else _pick(seq_rows, (1056, 1024, 512, 384, 256, 128, 64, 32, 16, 8))
    ni = m // tm
    in_specs = [pl.BlockSpec((tm, d), lambda i, j: (i, 0)),
                pl.BlockSpec((d, tn), lambda i, j: (0, j)),
                pl.BlockSpec((d, tn), lambda i, j: (0, j + nj)),
                pl.BlockSpec((3, tn), lambda i, j: (0, j)),
                pl.BlockSpec((1, tn), lambda i, j: (0, j))]
    args = [x1b, w_ffn_in, w_ffn_in, conv_w, conv_b.reshape(1, dff)]
    tile = pl.BlockSpec((tm, tn), lambda i, j: (i, j))
    if sample:
        in_specs += [tile, tile]
        args += [p1, p2]
        out_specs = [tile, tile]
        out_shape = [jax.ShapeDtypeStruct((m, dff), BF16), jax.ShapeDtypeStruct((m, dff), F32)]
        scratch = []
    else:
        out_specs = [tile, pl.BlockSpec((8, tn), lambda i, j: (i, j))]
        out_shape = [jax.ShapeDtypeStruct((m, dff), BF16), jax.ShapeDtypeStruct((ni * 8, dff), F32)]
        scratch = [pltpu.VMEM((8, dff), F32)]
    return pl.pallas_call(
        functools.partial(_ffn_in_kernel, tm=tm, tn=tn, seq_rows=seq_rows, pad=pad, sample=sample),
        grid=(ni, nj),
        in_specs=in_specs,
        out_specs=out_specs,
        out_shape=out_shape,
        scratch_shapes=scratch,
        compiler_params=_cparams(("arbitrary", "arbitrary")),
        name="ffn_in_sample" if sample else "ffn_in_prompt",
    )(*args)


def _layer_params(lw, d_model):
    (w_in, b_f, mu_shift, w0, w_decay_up, a0, w_iclr_up, w_gate_up, k_k, k_a, r_k,
     gn_g, gn_b, w_out, ln1_g, ln1_b, w_ffn_in, conv_w, conv_b, w_ffn_out, ln2_g, ln2_b) = lw
    dr = w0.shape[-1]
    hf = b_f.shape[-1]
    df = hf * HEAD_DIM
    d_dec, d_icl, d_gate = w_decay_up.shape[0], w_iclr_up.shape[0], w_gate_up.shape[0]
    rw_cols = 3 * dr + d_dec + d_icl + d_gate
    n_lora = d_dec + d_icl + d_gate
    lora_w = -(-(n_lora + hf) // (4 * LANES)) * (4 * LANES)
    assert (3 * dr + 3 * df) % lora_w == 0 and dr % LANES == 0 and df % LANES == 0
    pk, pv = _rwkv_col_orders(dr // HEAD_DIM)
    on_k = lambda a: a[..., pk]
    on_v = lambda a: a[..., pv]
    fox0 = rw_cols
    w_perm = jnp.concatenate(
        [on_k(w_in[:, :dr]), on_k(w_in[:, dr:2 * dr]), on_v(w_in[:, 2 * dr:3 * dr]),
         w_in[:, fox0:fox0 + 3 * df], w_in[:, 3 * dr:rw_cols],
         w_in[:, fox0 + 3 * df:], jnp.zeros((d_model, lora_w - n_lora - hf), w_in.dtype)], axis=1).astype(BF16)

    def pad_rows(w, off):
        return jnp.zeros((lora_w, dr), F32).at[off:off + w.shape[0]].set(w).astype(BF16)

    w_out_perm = jnp.concatenate([w_out[:dr][pv], w_out[dr:]], axis=0).astype(BF16)
    prm = dict(
        dr=dr, df=df, lora_w=lora_w, f_off=n_lora, pk=pk, pv=pv,
        mu_rkv=jnp.concatenate([on_k(mu_shift[:dr]), on_k(mu_shift[dr:2 * dr]),
                                on_v(mu_shift[2 * dr:3 * dr])]).reshape(1, -1),
        mu_lora=jnp.zeros((1, lora_w), F32).at[0, :n_lora].set(mu_shift[3 * dr:]),
        wd=pad_rows(on_k(w_decay_up), 0), wa=pad_rows(on_k(w_iclr_up), d_dec),
        wg=pad_rows(on_v(w_gate_up), d_dec + d_icl),
        w0=on_k(w0).reshape(1, dr), a0=on_k(a0).reshape(1, dr), k_k=on_k(k_k).reshape(1, dr),
        k_a=on_k(k_a).reshape(1, dr), r_k=on_k(r_k.reshape(dr)).reshape(1, dr), w_perm=w_perm,
        w_out=w_out_perm, w_ffn_in=w_ffn_in, w_ffn_out=w_ffn_out.astype(BF16), b_f=b_f,
        gn_g=on_v(gn_g), gn_b=on_v(gn_b), ln1=(ln1_g, ln1_b), ln2=(ln2_g, ln2_b), conv_w=conv_w, conv_b=conv_b)
    return prm


def _mm_tiles(m, n, seq_rows):
    tm = _pick(seq_rows, (1056, 1024, 768, 512, 384, 256, 128, 64, 32, 16, 8)) if m > 512 else m
    tn = _pick(n, (1024, 512, 256, 128))
    return tm, tn


def _permute_rw_row(row, prm):
    dr = prm["dr"]
    n_lora = prm["f_off"]
    lora = jnp.zeros(row.shape[:-1] + (prm["lora_w"],), F32).at[..., :n_lora].set(row[..., 3 * dr:])
    rkv = jnp.concatenate([row[..., :dr][..., prm["pk"]], row[..., dr:2 * dr][..., prm["pk"]],
                           row[..., 2 * dr:3 * dr][..., prm["pv"]]], axis=-1)
    return rkv, lora


def _unpermute_rkv(rkv, prm):
    dr = prm["dr"]
    ik, iv = np.argsort(prm["pk"]), np.argsort(prm["pv"])
    return jnp.concatenate([rkv[..., :dr][..., ik], rkv[..., dr:2 * dr][..., ik],
                            rkv[..., 2 * dr:3 * dr][..., iv]], axis=-1)


def _run_group(x2, ln0, prm, alpha, *, n_seq, seq_rows, pad, skip, sample, shift_prev, s0, ffn_prev, attend):
    ln0_g, ln0_b = ln0
    m, d = x2.shape
    dr, df = prm["dr"], prm["df"]
    h_b = _ln0(x2, ln0_g, ln0_b)
    tm, tn = _mm_tiles(m, prm["w_perm"].shape[1], seq_rows)
    p = _matmul(h_b, prm["w_perm"], tm, tn, "in_proj")

    p3 = p.reshape(n_seq, seq_rows, -1)
    if sample:
        sp_rkv, sp_lora = _permute_rw_row(shift_prev, prm)
        expand = lambda a: jnp.zeros((n_seq, seq_rows, a.shape[-1]), F32).at[:, 0].set(a)
        sp_rkv, sp_lora = expand(sp_rkv), expand(sp_lora)
    else:
        sp_rkv = sp_lora = None
    r, w, k, v, kk, b, g, bonus = _rwkv_prep(p3, sp_rkv, sp_lora, prm, pad=pad, sample=sample)
    o_scan, s_last = _rwkv_scan(r, w, k, kk, b, v, s0)
    o_rw = _rwkv_post(o_scan, g, bonus, prm["gn_g"], prm["gn_b"]).reshape(m, dr)

    lf, c = _logf_cumsum(p, prm["b_f"], prm, n_seq, seq_rows, pad)
    o_fox = attend(p, c)

    tm, tn = _mm_tiles(m, d, seq_rows)
    mix = _matmul2(o_rw, o_fox, prm["w_out"], tm, tn, "out_proj")
    x1, x1b = _ln1(x2, mix, ln0_g, ln0_b, *prm["ln1"], alpha)

    if sample:
        p1 = jnp.zeros((n_seq, seq_rows, ffn_prev.shape[-1]), F32).at[:, 0].set(ffn_prev[:, 1])
        p2 = p1.at[:, 0].set(ffn_prev[:, 0]).at[:, 1].set(ffn_prev[:, 1])
        p1, p2 = p1.reshape(m, -1), p2.reshape(m, -1)
    else:
        p1 = p2 = None
    hid, u_aux = _ffn_in(x1b, prm["w_ffn_in"], prm["conv_w"], prm["conv_b"], p1, p2,
                         seq_rows=seq_rows, pad=pad, sample=sample)
    dff = hid.shape[1]
    tm_o = _pick(seq_rows, (528, 512, 384, 256, 128, 64, 32, 16, 8)) if m > 512 else m
    ffn = _matmul(hid, prm["w_ffn_out"], tm_o, _pick(d, (512, 256, 128)), "ffn_out")
    y = _ln2(x1, ffn, *prm["ln2"], alpha, n_seq, seq_rows, skip)

    hf = df // HEAD_DIM
    k_new = p3[:, pad:, 3 * dr + df:3 * dr + 2 * df].reshape(n_seq, seq_rows - pad, hf, HEAD_DIM)
    v_new = p3[:, pad:, 3 * dr + 2 * df:3 * dr + 3 * df].reshape(n_seq, seq_rows - pad, hf, HEAD_DIM)
    last = p3[:, -1]
    shift_last = jnp.concatenate([_unpermute_rkv(last[:, :3 * dr], prm),
                                  last[:, 3 * dr + 3 * df:3 * dr + 3 * df + prm["f_off"]]], axis=1)
    if sample:
        conv_state = u_aux.reshape(n_seq, seq_rows, dff)[:, -2:]
    else:
        tiles_per_seq = u_aux.shape[0] // 8 // n_seq
        conv_state = u_aux.reshape(n_seq, tiles_per_seq, 8, dff)[:, -1, -2:]
    return y, (k_new, v_new, lf[:, pad:], s_last, shift_last, conv_state)


def kernel(x_prompt, x_sample, cache_k, cache_v, cache_logf, state_rwkv, state_shift, state_ffn_conv,
           page_table, meta_tokens, ln0_g, ln0_b, w_in, b_f, mu_shift, w0, w_decay_up, a0, w_iclr_up,
           w_gate_up, k_k, k_a, r_k, gn_g, gn_b, w_out, ln1_g, ln1_b, w_ffn_in, conv_w, conv_b,
           w_ffn_out, ln2_g, ln2_b):
    depth = w_in.shape[0]
    assert depth == 1, "the token layout between layers is only wired for a single layer"
    b, seq, d = x_prompt.shape
    db, nq, _ = x_sample.shape
    n_meta = meta_tokens.shape[0]
    alpha = (2 * depth) ** 0.25
    n_pool, page = cache_k.shape[1:3]
    n_pages = page_table.shape[1]
    h_rw = r_k.shape[1]

    t_real = seq + n_meta
    pad = (-t_real) % Q_BLOCK
    t_pad = t_real + pad
    meta = jnp.broadcast_to(meta_tokens[None].astype(x_prompt.dtype), (b, n_meta, d))
    xp = jnp.concatenate([jnp.zeros((b, pad, d), x_prompt.dtype), meta, x_prompt], axis=1).reshape(b * t_pad, d)
    xs = x_sample.reshape(db * nq, d)
    pt_flat = page_table.reshape(-1).astype(jnp.int32)

    outs_p, outs_s = [], []
    for l in range(depth):
        lw = tuple(wt[l] for wt in (w_in, b_f, mu_shift, w0, w_decay_up, a0, w_iclr_up, w_gate_up, k_k, k_a,
                                     r_k, gn_g, gn_b, w_out, ln1_g, ln1_b, w_ffn_in, conv_w, conv_b,
                                     w_ffn_out, ln2_g, ln2_b))
        prm = _layer_params(lw, d)
        hf = prm["df"] // HEAD_DIM

        attend_p = lambda p, c: _fox_prompt(p, c, prm, b, t_pad, pad)
        y_p, st_p = _run_group(xp, (ln0_g, ln0_b), prm, alpha, n_seq=b, seq_rows=t_pad, pad=pad, skip=pad + n_meta, sample=False,
                               shift_prev=None, s0=jnp.zeros((b, h_rw, HEAD_DIM, HEAD_DIM), F32),
                               ffn_prev=None, attend=attend_p)

        cache_kt = cache_k.transpose(0, 1, 3, 4, 2).reshape(depth, n_pool, hf * HEAD_DIM, page)
        cache_vt = cache_v.transpose(0, 1, 3, 4, 2).reshape(depth, n_pool, hf * HEAD_DIM, page)
        cache_lft = cache_logf.astype(F32).transpose(0, 1, 3, 2)
        attend_s = lambda p, c: _fox_sample(p, c, cache_kt, cache_vt, cache_lft, pt_flat, prm, db, nq, n_pages, l)
        y_s, st_s = _run_group(xs, (ln0_g, ln0_b), prm, alpha, n_seq=db, seq_rows=nq, pad=0, skip=0, sample=True,
                               shift_prev=state_shift[l], s0=state_rwkv[l], ffn_prev=state_ffn_conv[l],
                               attend=attend_s)
        outs_p.append(st_p)
        outs_s.append(st_s)

    k_p, v_p, lf_p, rw_p, sh_p, cv_p = (jnp.stack([o[i] for o in outs_p]) for i in range(6))
    k_s, v_s, lf_s, rw_s, sh_s, cv_s = (jnp.stack([o[i] for o in outs_s]) for i in range(6))
    y_prompt = y_p
    y_sample = y_s.reshape(db, nq, d)
    return (y_prompt, y_sample, k_p, v_p, lf_p, rw_p, sh_p, cv_p, k_s, v_s, lf_s, rw_s, sh_s, cv_s)
```
